```python
import jax, jax.numpy as jnp
from jax import lax
import numpy as np

D_MODEL = 1024
BATCH = 8
SEQ = 4096
DEPTH = 1

D_LRU = D_MODEL
LRU_HEADS = 16
LRU_HEAD_DIM = D_LRU // LRU_HEADS
LRU_CONV = 4
LRU_C = 8.0
D_CONF = D_MODEL
CONF_KERNEL = 31
D_FF = 3 * D_MODEL
FFN_CONV = 3
EPS = 1e-6
IN_SIZES = (D_LRU, D_LRU, D_CONF, D_CONF, D_MODEL, D_MODEL)
IN_SPLITS = tuple(int(s) for s in np.cumsum(IN_SIZES)[:-1])
D_IN = int(sum(IN_SIZES))

kernel_name = "hybrid_rglru_conformer_convffn"


def rmsnorm(x, g):
    x32 = x.astype(jnp.float32)
    y = x32 * lax.rsqrt(jnp.mean(x32 * x32, axis=-1, keepdims=True) + EPS)
    return (y * g.astype(jnp.float32)).astype(x.dtype)


def layernorm(x, g, b):
    x32 = x.astype(jnp.float32)
    mu = jnp.mean(x32, axis=-1, keepdims=True)
    xc = x32 - mu
    var = jnp.mean(xc * xc, axis=-1, keepdims=True)
    y = xc * lax.rsqrt(var + EPS)
    return (y * g.astype(jnp.float32) + b.astype(jnp.float32)).astype(x.dtype)


def causal_dwconv(x, w, b):
    k = w.shape[0]
    y = lax.conv_general_dilated(
        x, w[:, None, :].astype(x.dtype), window_strides=(1,), padding=[(k - 1, 0)],
        dimension_numbers=("NWC", "WIO", "NWC"), feature_group_count=x.shape[-1])
    return y + b.astype(x.dtype)


def _linear_rec_combine(left, right):
    a_l, b_l = left
    a_r, b_r = right
    return a_l * a_r, a_r * b_l + b_r


def rg_lru(xc, wa, ba, wx, bx, lam):
    bsz, s, _ = xc.shape
    xh = xc.reshape(bsz, s, LRU_HEADS, LRU_HEAD_DIM)
    r = jax.nn.sigmoid(jnp.einsum("bshd,hde->bshe", xh, wa).reshape(bsz, s, D_LRU) + ba)
    i = jax.nn.sigmoid(jnp.einsum("bshd,hde->bshe", xh, wx).reshape(bsz, s, D_LRU) + bx)
    log_a = -LRU_C * r.astype(jnp.float32) * jax.nn.softplus(-lam.astype(jnp.float32))
    a = jnp.exp(log_a)
    mult = jnp.sqrt(jnp.maximum(-jnp.expm1(2.0 * log_a), 0.0))
    u = mult * (i * xc).astype(jnp.float32)
    _, h = lax.associative_scan(_linear_rec_combine, (a, u), axis=1)
    return h.astype(xc.dtype)


def setup_inputs(seed: int = 0) -> dict:
    key = jax.random.key(seed)
    ks = jax.random.split(key, 32)
    f32 = jnp.float32

    def nrm(k, shape, fan_in):
        return jax.random.normal(k, shape, f32) * (fan_in ** -0.5)

    def gain(k, shape):
        return 1.0 + 0.02 * jax.random.normal(k, shape, f32)

    def small(k, shape):
        return 0.02 * jax.random.normal(k, shape, f32)

    L = DEPTH
    a0 = jax.random.uniform(ks[9], (L, D_LRU), f32, minval=0.9, maxval=0.999)
    base = a0 ** (1.0 / LRU_C)
    lam = jnp.log(base) - jnp.log1p(-base)
    return {
        "x": jax.random.normal(ks[0], (BATCH, SEQ, D_MODEL), f32),
        "g_mix": gain(ks[1], (L, D_MODEL)),
        "w_in": nrm(ks[2], (L, D_MODEL, D_IN), D_MODEL),
        "lru_conv_w": nrm(ks[3], (L, LRU_CONV, D_LRU), LRU_CONV),
        "lru_conv_b": small(ks[4], (L, D_LRU)),
        "lru_wa": nrm(ks[5], (L, LRU_HEADS, LRU_HEAD_DIM, LRU_HEAD_DIM), LRU_HEAD_DIM),
        "lru_ba": small(ks[6], (L, D_LRU)),
        "lru_wx": nrm(ks[7], (L, LRU_HEADS, LRU_HEAD_DIM, LRU_HEAD_DIM), LRU_HEAD_DIM),
        "lru_bx": small(ks[8], (L, D_LRU)),
        "lru_lambda": lam,
        "w_lru_out": nrm(ks[10], (L, D_LRU, D_MODEL), D_LRU),
        "conf_dw_w": nrm(ks[11], (L, CONF_KERNEL, D_CONF), CONF_KERNEL),
        "conf_dw_b": small(ks[12], (L, D_CONF)),
        "conf_ln_g": gain(ks[13], (L, D_CONF)),
        "conf_ln_b": small(ks[14], (L, D_CONF)),
        "w_conf_out": nrm(ks[15], (L, D_CONF, D_MODEL), D_CONF),
        "b_gate": small(ks[16], (L, 2 * D_MODEL)),
        "w_out": nrm(ks[17], (L, D_MODEL, D_MODEL), D_MODEL),
        "g_ffn": gain(ks[18], (L, D_MODEL)),
        "w_up": nrm(ks[19], (L, D_MODEL, 2 * D_FF), D_MODEL),
        "ffn_dw_w": nrm(ks[20], (L, FFN_CONV, D_FF), FFN_CONV),
        "ffn_dw_b": small(ks[21], (L, D_FF)),
        "w_down": nrm(ks[22], (L, D_FF, D_MODEL), D_FF),
        "g_final": gain(ks[23], (D_MODEL,)),
    }


def reference(x, g_mix, w_in, lru_conv_w, lru_conv_b, lru_wa, lru_ba, lru_wx, lru_bx,
              lru_lambda, w_lru_out, conf_dw_w, conf_dw_b, conf_ln_g, conf_ln_b, w_conf_out,
              b_gate, w_out, g_ffn, w_up, ffn_dw_w, ffn_dw_b, w_down, g_final):
    for l in range(DEPTH):
        h = rmsnorm(x, g_mix[l])
        u = jnp.einsum("bsd,de->bse", h, w_in[l])
        xa, ga, ca, cb, sa, sb = jnp.split(u, IN_SPLITS, axis=-1)

        xa = causal_dwconv(xa, lru_conv_w[l], lru_conv_b[l])
        ha = rg_lru(xa, lru_wa[l], lru_ba[l], lru_wx[l], lru_bx[l], lru_lambda[l])
        y_a = jnp.einsum("bsc,cd->bsd", ha * jax.nn.gelu(ga, approximate=True), w_lru_out[l])

        cg = ca * jax.nn.sigmoid(cb)
        cg = causal_dwconv(cg, conf_dw_w[l], conf_dw_b[l])
        cg = jax.nn.silu(layernorm(cg, conf_ln_g[l], conf_ln_b[l]))
        y_b = jnp.einsum("bsc,cd->bsd", cg, w_conf_out[l])

        gates = jax.nn.sigmoid(jnp.concatenate([sa, sb], axis=-1) + b_gate[l])
        gate_a, gate_b = jnp.split(gates, 2, axis=-1)
        merged = gate_a * y_a + gate_b * y_b
        x = x + jnp.einsum("bsd,de->bse", merged, w_out[l])

        h = rmsnorm(x, g_ffn[l])
        up = jnp.einsum("bsd,df->bsf", h, w_up[l])
        g, v = jnp.split(up, 2, axis=-1)
        g = causal_dwconv(g, ffn_dw_w[l], ffn_dw_b[l])
        x = x + jnp.einsum("bsf,fd->bsd", jax.nn.gelu(g, approximate=True) * v, w_down[l])

    return rmsnorm(x, g_final)
```

```python
import functools

import jax
import jax.numpy as jnp
from jax import lax
from jax.experimental import pallas as pl
from jax.experimental.pallas import tpu as pltpu

D_MODEL = 1024
BATCH = 8
SEQ = 4096
LRU_HEADS = 16
LRU_HEAD_DIM = D_MODEL // LRU_HEADS
LRU_CONV = 4
LRU_C = 8.0
CONF_KERNEL = 31
D_FF = 3 * D_MODEL
FFN_CONV = 3
EPS = 1e-6

F32 = jnp.float32
BF16 = jnp.bfloat16

TT = 64
M = TT * BATCH
RB = 16
GATE_GROUP = 256
N_GATE_GROUPS = D_MODEL // GATE_GROUP
HALO_A = (LRU_CONV - 1) * BATCH
HALO_B = (CONF_KERNEL - 1) * BATCH
HALO_F = (FFN_CONV - 1) * BATCH
VMEM_LIMIT_BYTES = 58 * 1024 * 1024


def _sigmoid(z):
    return 0.5 * jnp.tanh(0.5 * z) + 0.5


def _gelu_tanh(z):
    c = 0.7978845608028654
    return 0.5 * z * (1.0 + jnp.tanh(c * (z + 0.044715 * (z * z * z))))


def _rows(i):
    return pl.ds(pl.multiple_of(i * RB, RB), RB)


def _mixer_kernel(x_ref, gmix_ref, win_ref, w4_ref, b4_ref, wg_ref, ba_ref, bx_ref, lam_ref,
                  wa_ref, w31_ref, b31_ref, lng_ref, lnb_ref, wb_ref, bgate_ref, wout_ref,
                  o_ref,
                  hb_ref, xa_ext, u_ref, cg_ext, gp_ref, ya_ref, yb_ref, hcar_ref):
    @pl.when(pl.program_id(0) == 0)
    def _():
        xa_ext[0:HALO_A, :] = jnp.zeros((HALO_A, D_MODEL), F32)
        cg_ext[0:HALO_B, :] = jnp.zeros((HALO_B, D_MODEL), F32)
        hcar_ref[...] = jnp.zeros((BATCH, D_MODEL), F32)

    def norm_body(i, c):
        rows = _rows(i)
        xv = x_ref[rows, :]
        ms = jnp.mean(xv * xv, axis=-1, keepdims=True)
        hb_ref[rows, :] = (xv * lax.rsqrt(ms + EPS) * gmix_ref[...]).astype(BF16)
        return c
    lax.fori_loop(0, M // RB, norm_body, 0)

    hb = hb_ref[...]
    xa_ext[HALO_A:HALO_A + M, :] = jnp.dot(hb, win_ref[:, 0:D_MODEL], preferred_element_type=F32)
    for g in range(5):
        u_ref[g] = jnp.dot(hb, win_ref[:, (g + 1) * D_MODEL:(g + 2) * D_MODEL],
                           preferred_element_type=F32)

    def conv4_body(i, c):
        rows = _rows(i)
        r0 = pl.multiple_of(i * RB, RB)
        acc = b4_ref[...] + w4_ref[0:1, :] * xa_ext[rows, :]
        for k in range(1, LRU_CONV):
            acc = acc + w4_ref[k:k + 1, :] * xa_ext[pl.ds(r0 + k * BATCH, RB), :]
        xa_ext[rows, :] = acc
        hb_ref[rows, :] = acc.astype(BF16)
        return c
    lax.fori_loop(0, M // RB, conv4_body, 0)

    for g in range(N_GATE_GROUPS):
        cols = slice(g * GATE_GROUP, (g + 1) * GATE_GROUP)
        gp = jnp.dot(hb_ref[:, cols], wg_ref[g], preferred_element_type=F32)
        gp_ref[0, :, cols] = gp[:, 0:GATE_GROUP]
        gp_ref[1, :, cols] = gp[:, GATE_GROUP:2 * GATE_GROUP]

    lam = lam_ref[...]
    softplus_neg_lam = jnp.maximum(-lam, 0.0) + jnp.log(1.0 + jnp.exp(-jnp.abs(lam)))
    log_a_scale = -LRU_C * softplus_neg_lam

    def lru_body(i, h):
        rows = _rows(i)
        r0 = pl.multiple_of(i * RB, RB)
        xc = xa_ext[rows, :]
        rg = _sigmoid(gp_ref[0, rows, :] + ba_ref[...])
        ig = _sigmoid(gp_ref[1, rows, :] + bx_ref[...])
        log_a = log_a_scale * rg
        a = jnp.exp(log_a)
        one_minus_a2 = -jnp.tanh(log_a) * (a * a + 1.0)
        mult = jnp.sqrt(jnp.maximum(one_minus_a2, 0.0))
        u = mult * (ig * xc)
        hs = []
        for s in range(RB // BATCH):
            h = a[s * BATCH:(s + 1) * BATCH, :] * h + u[s * BATCH:(s + 1) * BATCH, :]
            hs.append(h)
        hh = jnp.concatenate(hs, axis=0)
        ya_ref[rows, :] = (hh * _gelu_tanh(u_ref[0, rows, :])).astype(BF16)
        cg_ext[pl.ds(HALO_B + r0, RB), :] = u_ref[1, rows, :] * _sigmoid(u_ref[2, rows, :])
        return h
    hcar_ref[...] = lax.fori_loop(0, M // RB, lru_body, hcar_ref[...])

    def conf_body(i, c):
        rows = _rows(i)
        r0 = pl.multiple_of(i * RB, RB)
        acc = b31_ref[...] + w31_ref[0:1, :] * cg_ext[rows, :]
        for k in range(1, CONF_KERNEL):
            acc = acc + w31_ref[k:k + 1, :] * cg_ext[pl.ds(r0 + k * BATCH, RB), :]
        mu = jnp.mean(acc, axis=-1, keepdims=True)
        cen = acc - mu
        var = jnp.mean(cen * cen, axis=-1, keepdims=True)
        y = cen * lax.rsqrt(var + EPS) * lng_ref[...] + lnb_ref[...]
        yb_ref[rows, :] = (y * _sigmoid(y)).astype(BF16)
        return c
    lax.fori_loop(0, M // RB, conf_body, 0)

    xa_ext[0:HALO_A, :] = xa_ext[M:M + HALO_A, :]
    cg_ext[0:HALO_B, :] = cg_ext[M:M + HALO_B, :]

    u_ref[0] = jnp.dot(ya_ref[...], wa_ref[...], preferred_element_type=F32)
    u_ref[1] = jnp.dot(yb_ref[...], wb_ref[...], preferred_element_type=F32)

    def merge_body(i, c):
        rows = _rows(i)
        gate_a = _sigmoid(u_ref[3, rows, :] + bgate_ref[:, 0:D_MODEL])
        gate_b = _sigmoid(u_ref[4, rows, :] + bgate_ref[:, D_MODEL:2 * D_MODEL])
        hb_ref[rows, :] = (gate_a * u_ref[0, rows, :] + gate_b * u_ref[1, rows, :]).astype(BF16)
        return c
    lax.fori_loop(0, M // RB, merge_body, 0)

    o_ref[...] = x_ref[...] + jnp.dot(hb_ref[...], wout_ref[...], preferred_element_type=F32)


def _ffn_kernel(x_ref, gffn_ref, wup_ref, w3_ref, b3_ref, wdown_ref, gfin_ref,
                o_ref,
                hb_ref, g_ext, v_ref, p_ref):
    @pl.when(pl.program_id(0) == 0)
    def _():
        g_ext[0:HALO_F, :] = jnp.zeros((HALO_F, D_FF), F32)

    def norm_body(i, c):
        rows = _rows(i)
        xv = x_ref[rows, :]
        ms = jnp.mean(xv * xv, axis=-1, keepdims=True)
        hb_ref[rows, :] = (xv * lax.rsqrt(ms + EPS) * gffn_ref[...]).astype(BF16)
        return c
    lax.fori_loop(0, M // RB, norm_body, 0)

    hb = hb_ref[...]
    g_ext[HALO_F:HALO_F + M, :] = jnp.dot(hb, wup_ref[:, 0:D_FF], preferred_element_type=F32)
    v_ref[...] = jnp.dot(hb, wup_ref[:, D_FF:2 * D_FF], preferred_element_type=F32)

    def act_body(i, c):
        rows = _rows(i)
        r0 = pl.multiple_of(i * RB, RB)
        acc = b3_ref[...] + w3_ref[0:1, :] * g_ext[rows, :]
        for k in range(1, FFN_CONV):
            acc = acc + w3_ref[k:k + 1, :] * g_ext[pl.ds(r0 + k * BATCH, RB), :]
        p_ref[rows, :] = (_gelu_tanh(acc) * v_ref[rows, :]).astype(BF16)
        return c
    lax.fori_loop(0, M // RB, act_body, 0)

    g_ext[0:HALO_F, :] = g_ext[M:M + HALO_F, :]

    o_ref[...] = jnp.dot(p_ref[...], wdown_ref[...], preferred_element_type=F32)

    def out_body(i, c):
        rows = _rows(i)
        xv = x_ref[rows, :] + o_ref[rows, :]
        ms = jnp.mean(xv * xv, axis=-1, keepdims=True)
        o_ref[rows, :] = xv * lax.rsqrt(ms + EPS) * gfin_ref[...]
        return c
    lax.fori_loop(0, M // RB, out_body, 0)


def _resident(shape):
    nd = len(shape)
    return pl.BlockSpec(shape, lambda t: (0,) * nd, pipeline_mode=pl.Buffered(1))


def _row_tile():
    return pl.BlockSpec((M, D_MODEL), lambda t: (t, 0))


def _block_diag_groups(w):
    hpg = GATE_GROUP // LRU_HEAD_DIM
    w4 = w.reshape(N_GATE_GROUPS, hpg, LRU_HEAD_DIM, LRU_HEAD_DIM)
    eye = jnp.eye(hpg, dtype=w.dtype)
    return jnp.einsum("ghde,hk->ghdke", w4, eye).reshape(N_GATE_GROUPS, GATE_GROUP, GATE_GROUP)


def _mixer_call(xt, gmix, win, w4, b4, wg, ba, bx, lam, wa, w31, b31, lng, lnb, wb, bgate, wout):
    n_rows = xt.shape[0]
    operands = (xt, gmix, win, w4, b4, wg, ba, bx, lam, wa, w31, b31, lng, lnb, wb, bgate, wout)
    in_specs = [_row_tile()] + [_resident(a.shape) for a in operands[1:]]
    return pl.pallas_call(
        _mixer_kernel,
        grid=(n_rows // M,),
        in_specs=in_specs,
        out_specs=_row_tile(),
        out_shape=jax.ShapeDtypeStruct((n_rows, D_MODEL), F32),
        scratch_shapes=[
            pltpu.VMEM((M, D_MODEL), BF16),
            pltpu.VMEM((M + HALO_A, D_MODEL), F32),
            pltpu.VMEM((5, M, D_MODEL), F32),
            pltpu.VMEM((M + HALO_B, D_MODEL), F32),
            pltpu.VMEM((2, M, D_MODEL), F32),
            pltpu.VMEM((M, D_MODEL), BF16),
            pltpu.VMEM((M, D_MODEL), BF16),
            pltpu.VMEM((BATCH, D_MODEL), F32),
        ],
        compiler_params=pltpu.CompilerParams(
            dimension_semantics=("arbitrary",), vmem_limit_bytes=VMEM_LIMIT_BYTES),
        name="mixer",
    )(*operands)


def _ffn_call(x1, gffn, wup, w3, b3, wdown, gfin):
    n_rows = x1.shape[0]
    operands = (x1, gffn, wup, w3, b3, wdown, gfin)
    in_specs = [_row_tile()] + [_resident(a.shape) for a in operands[1:]]
    return pl.pallas_call(
        _ffn_kernel,
        grid=(n_rows // M,),
        in_specs=in_specs,
        out_specs=_row_tile(),
        out_shape=jax.ShapeDtypeStruct((n_rows, D_MODEL), F32),
        scratch_shapes=[
            pltpu.VMEM((M, D_MODEL), BF16),
            pltpu.VMEM((M + HALO_F, D_FF), F32),
            pltpu.VMEM((M, D_FF), F32),
            pltpu.VMEM((M, D_FF), BF16),
        ],
        compiler_params=pltpu.CompilerParams(
            dimension_semantics=("arbitrary",), vmem_limit_bytes=VMEM_LIMIT_BYTES),
        name="ffn",
    )(*operands)


def kernel(x, g_mix, w_in, lru_conv_w, lru_conv_b, lru_wa, lru_ba, lru_wx, lru_bx, lru_lambda,
           w_lru_out, conf_dw_w, conf_dw_b, conf_ln_g, conf_ln_b, w_conf_out, b_gate, w_out,
           g_ffn, w_up, ffn_dw_w, ffn_dw_b, w_down, g_final):
    bsz, seq, d = x.shape
    assert (bsz, seq, d) == (BATCH, SEQ, D_MODEL)
    depth = g_mix.shape[0]
    row = lambda v: v.reshape(1, -1)

    xt = jnp.transpose(x, (1, 0, 2)).reshape(seq * bsz, d)
    for l in range(depth):
        wg = jnp.concatenate(
            [_block_diag_groups(lru_wa[l]), _block_diag_groups(lru_wx[l])], axis=-1).astype(BF16)
        xt = _mixer_call(
            xt, row(g_mix[l]), w_in[l].astype(BF16), lru_conv_w[l], row(lru_conv_b[l]), wg,
            row(lru_ba[l]), row(lru_bx[l]), row(lru_lambda[l]), w_lru_out[l].astype(BF16),
            conf_dw_w[l], row(conf_dw_b[l]), row(conf_ln_g[l]), row(conf_ln_b[l]),
            w_conf_out[l].astype(BF16), row(b_gate[l]), w_out[l].astype(BF16))
        last = l == depth - 1
        assert last, "only the last layer's FFN kernel applies the final norm"
        xt = _ffn_call(
            xt, row(g_ffn[l]), w_up[l].astype(BF16), ffn_dw_w[l], row(ffn_dw_b[l]),
            w_down[l].astype(BF16), row(g_final))
    return jnp.transpose(xt.reshape(seq, bsz, d), (1, 0, 2))
```

```python
import jax
import jax.numpy as jnp
from jax import lax
from jax.experimental import pallas as pl
from jax.experimental.pallas import tpu as pltpu

D_MODEL = 1024
BATCH = 8
SEQ = 4096
LRU_HEADS = 16
LRU_HEAD_DIM = D_MODEL // LRU_HEADS
LRU_CONV = 4
LRU_C = 8.0
CONF_KERNEL = 31
D_FF = 3 * D_MODEL
FFN_CONV = 3
EPS = 1e-6

F32 = jnp.float32
BF16 = jnp.bfloat16

LANES = 128
TT = 64
M = TT * BATCH
RBN = 128
SUB = 64
COLB = 256
NB_D = D_MODEL // COLB
NB_F = D_FF // COLB
DOWN_K = 1024
TB31 = 16
HALO_A = (LRU_CONV - 1) * BATCH
HALO_B = (CONF_KERNEL - 1) * BATCH
HALO_F = (FFN_CONV - 1) * BATCH
VMEM_LIMIT_BYTES = 58 * 1024 * 1024

U_GA, U_CA, U_CB, U_SA, U_SB = range(5)
U_YA, U_YB, U_C31 = U_GA, U_CA, U_CB


def _sigmoid(z):
    return 0.5 * jnp.tanh(0.5 * z) + 0.5


def _gelu_tanh(z):
    c = 0.7978845608028654
    return 0.5 * z * (1.0 + jnp.tanh(c * (z + 0.044715 * (z * z * z))))


def _nrows(i):
    return pl.ds(pl.multiple_of(i * RBN, RBN), RBN)


def _colblock(j):
    return slice(j * COLB, (j + 1) * COLB)


def _mixer_kernel(x_ref, gmix_ref, win_ref, w4_ref, b4_ref, wg_ref, ba_ref, bx_ref, lam_ref,
                  wa_ref, w31_ref, b31_ref, lng_ref, lnb_ref, wb_ref, bgate_ref, wout_ref,
                  o_ref,
                  hb_ref, xcb_ref, xa_ext, u_ref, cg_ext, gp_ref, ya_ref, yb_ref, hcar_ref):
    @pl.when(pl.program_id(0) == 0)
    def _():
        xa_ext[0:HALO_A, :] = jnp.zeros((HALO_A, D_MODEL), F32)
        cg_ext[0:HALO_B, :] = jnp.zeros((HALO_B, D_MODEL), F32)
        hcar_ref[...] = jnp.zeros((BATCH, D_MODEL), F32)

    def norm_body(i, c):
        rows = _nrows(i)
        xv = x_ref[rows, :]
        ms = jnp.mean(xv * xv, axis=-1, keepdims=True)
        hb_ref[rows, :] = (xv * lax.rsqrt(ms + EPS) * gmix_ref[...]).astype(BF16)
        return c
    lax.fori_loop(0, M // RBN, norm_body, 0)

    lam = lam_ref[...]
    softplus_neg_lam = jnp.maximum(-lam, 0.0) + jnp.log(1.0 + jnp.exp(-jnp.abs(lam)))
    log_a_scale = -LRU_C * softplus_neg_lam

    def in_proj_block(c):
        cols = _colblock(c)
        hb = hb_ref[...]
        xa_ext[HALO_A:HALO_A + M, cols] = jnp.dot(hb, win_ref[:, cols], preferred_element_type=F32)
        for g in range(5):
            wcols = slice((g + 1) * D_MODEL + c * COLB, (g + 1) * D_MODEL + (c + 1) * COLB)
            u_ref[g, :, cols] = jnp.dot(hb, win_ref[:, wcols], preferred_element_type=F32)

    def conv4_block(c):
        cols = _colblock(c)
        for s in range(M // SUB):
            r = s * SUB
            acc = b4_ref[:, cols] + w4_ref[0:1, cols] * xa_ext[r:r + SUB, cols]
            for k in range(1, LRU_CONV):
                acc = acc + w4_ref[k:k + 1, cols] * xa_ext[r + k * BATCH:r + k * BATCH + SUB, cols]
            xa_ext[r:r + SUB, cols] = acc
            xcb_ref[r:r + SUB, cols] = acc.astype(BF16)

    def gate_block(c):
        cols = _colblock(c)
        gp = jnp.dot(xcb_ref[:, cols], wg_ref[c], preferred_element_type=F32)
        gp_ref[0, :, cols] = gp[:, 0:COLB]
        gp_ref[1, :, cols] = gp[:, COLB:2 * COLB]

    def lru_block(c):
        cols = _colblock(c)
        h = hcar_ref[:, cols]
        scale = log_a_scale[:, cols]
        for s in range(M // SUB):
            rows = slice(s * SUB, (s + 1) * SUB)
            xc = xa_ext[rows, cols]
            rg = _sigmoid(gp_ref[0, rows, cols] + ba_ref[:, cols])
            ig = _sigmoid(gp_ref[1, rows, cols] + bx_ref[:, cols])
            log_a = scale * rg
            a = jnp.exp(log_a)
            one_minus_a2 = -jnp.tanh(log_a) * (a * a + 1.0)
            mult = jnp.sqrt(jnp.maximum(one_minus_a2, 0.0))
            u = mult * (ig * xc)
            hs = []
            for q in range(SUB // BATCH):
                h = a[q * BATCH:(q + 1) * BATCH, :] * h + u[q * BATCH:(q + 1) * BATCH, :]
                hs.append(h)
            hh = jnp.concatenate(hs, axis=0)
            ya_ref[rows, cols] = (hh * _gelu_tanh(u_ref[U_GA, rows, cols])).astype(BF16)
            cg_ext[HALO_B + s * SUB:HALO_B + (s + 1) * SUB, cols] = (
                u_ref[U_CA, rows, cols] * _sigmoid(u_ref[U_CB, rows, cols]))
        hcar_ref[:, cols] = h

    def conv31_block(c):
        for lc in range(COLB // LANES):
            lanes = slice(c * COLB + lc * LANES, c * COLB + (lc + 1) * LANES)
            taps = [w31_ref[k * BATCH:(k + 1) * BATCH, lanes] for k in range(CONF_KERNEL)]
            bias = b31_ref[:, lanes]
            for tb in range(TT // TB31):
                t0 = tb * TB31
                xs = [cg_ext[(t0 + i) * BATCH:(t0 + i + 1) * BATCH, lanes]
                      for i in range(TB31 + CONF_KERNEL - 1)]
                for t in range(TB31):
                    acc = bias + taps[0] * xs[t]
                    for k in range(1, CONF_KERNEL):
                        acc = acc + taps[k] * xs[t + k]
                    u_ref[U_C31, (t0 + t) * BATCH:(t0 + t + 1) * BATCH, lanes] = acc

    for c in range(NB_D):
        in_proj_block(c)
        conv4_block(c)
        gate_block(c)
        lru_block(c)
        conv31_block(c)

    xa_ext[0:HALO_A, :] = xa_ext[M:M + HALO_A, :]
    cg_ext[0:HALO_B, :] = cg_ext[M:M + HALO_B, :]

    u_ref[U_YA] = jnp.dot(ya_ref[...], wa_ref[...], preferred_element_type=F32)

    for s in range(M // SUB):
        rows = slice(s * SUB, (s + 1) * SUB)
        acc = u_ref[U_C31, rows, :]
        mu = jnp.mean(acc, axis=-1, keepdims=True)
        cen = acc - mu
        var = jnp.mean(cen * cen, axis=-1, keepdims=True)
        y = cen * lax.rsqrt(var + EPS) * lng_ref[...] + lnb_ref[...]
        yb_ref[rows, :] = (y * _sigmoid(y)).astype(BF16)

    u_ref[U_YB] = jnp.dot(yb_ref[...], wb_ref[...], preferred_element_type=F32)

    for s in range(M // SUB):
        rows = slice(s * SUB, (s + 1) * SUB)
        gate_a = _sigmoid(u_ref[U_SA, rows, :] + bgate_ref[:, 0:D_MODEL])
        gate_b = _sigmoid(u_ref[U_SB, rows, :] + bgate_ref[:, D_MODEL:2 * D_MODEL])
        hb_ref[rows, :] = (gate_a * u_ref[U_YA, rows, :] + gate_b * u_ref[U_YB, rows, :]).astype(BF16)

    o_ref[...] = x_ref[...] + jnp.dot(hb_ref[...], wout_ref[...], preferred_element_type=F32)


def _ffn_kernel(x_ref, gffn_ref, wup_ref, w3_ref, b3_ref, wdown_ref, gfin_ref,
                o_ref,
                hb_ref, g_ext, v_ref, p_ref):
    @pl.when(pl.program_id(0) == 0)
    def _():
        g_ext[0:HALO_F, :] = jnp.zeros((HALO_F, D_FF), F32)

    def norm_body(i, c):
        rows = _nrows(i)
        xv = x_ref[rows, :]
        ms = jnp.mean(xv * xv, axis=-1, keepdims=True)
        hb_ref[rows, :] = (xv * lax.rsqrt(ms + EPS) * gffn_ref[...]).astype(BF16)
        o_ref[rows, :] = xv
        return c
    lax.fori_loop(0, M // RBN, norm_body, 0)

    def up_block(j):
        cols = _colblock(j)
        vcols = slice(D_FF + j * COLB, D_FF + (j + 1) * COLB)
        hb = hb_ref[...]
        g_ext[HALO_F:HALO_F + M, cols] = jnp.dot(hb, wup_ref[:, cols], preferred_element_type=F32)
        v_ref[:, cols] = jnp.dot(hb, wup_ref[:, vcols], preferred_element_type=F32)

    def act_block(j):
        cols = _colblock(j)
        for s in range(M // SUB):
            r = s * SUB
            acc = b3_ref[:, cols] + w3_ref[0:1, cols] * g_ext[r:r + SUB, cols]
            for k in range(1, FFN_CONV):
                acc = acc + w3_ref[k:k + 1, cols] * g_ext[r + k * BATCH:r + k * BATCH + SUB, cols]
            p_ref[r:r + SUB, cols] = (_gelu_tanh(acc) * v_ref[r:r + SUB, cols]).astype(BF16)
        g_ext[0:HALO_F, cols] = g_ext[M:M + HALO_F, cols]

    def down_chunk(c):
        rows = slice(c * DOWN_K, (c + 1) * DOWN_K)
        o_ref[...] += jnp.dot(p_ref[:, rows], wdown_ref[rows, :], preferred_element_type=F32)

    for j in range(NB_F):
        up_block(j)
        act_block(j)
        if (j + 1) % (DOWN_K // COLB) == 0:
            down_chunk(j // (DOWN_K // COLB))

    def out_body(i, c):
        rows = _nrows(i)
        xv = o_ref[rows, :]
        ms = jnp.mean(xv * xv, axis=-1, keepdims=True)
        o_ref[rows, :] = xv * lax.rsqrt(ms + EPS) * gfin_ref[...]
        return c
    lax.fori_loop(0, M // RBN, out_body, 0)


def _resident(shape):
    nd = len(shape)
    return pl.BlockSpec(shape, lambda t: (0,) * nd, pipeline_mode=pl.Buffered(1))


def _row_tile():
    return pl.BlockSpec((M, D_MODEL), lambda t: (t, 0))


def _block_diag_groups(w):
    hpg = COLB // LRU_HEAD_DIM
    w4 = w.reshape(NB_D, hpg, LRU_HEAD_DIM, LRU_HEAD_DIM)
    eye = jnp.eye(hpg, dtype=w.dtype)
    return jnp.einsum("ghde,hk->ghdke", w4, eye).reshape(NB_D, COLB, COLB)


def _mixer_call(xt, gmix, win, w4, b4, wg, ba, bx, lam, wa, w31, b31, lng, lnb, wb, bgate, wout):
    n_rows = xt.shape[0]
    operands = (xt, gmix, win, w4, b4, wg, ba, bx, lam, wa, w31, b31, lng, lnb, wb, bgate, wout)
    in_specs = [_row_tile()] + [_resident(a.shape) for a in operands[1:]]
    return pl.pallas_call(
        _mixer_kernel,
        grid=(n_rows // M,),
        in_specs=in_specs,
        out_specs=_row_tile(),
        out_shape=jax.ShapeDtypeStruct((n_rows, D_MODEL), F32),
        scratch_shapes=[
            pltpu.VMEM((M, D_MODEL), BF16),
            pltpu.VMEM((M, D_MODEL), BF16),
            pltpu.VMEM((M + HALO_A, D_MODEL), F32),
            pltpu.VMEM((5, M, D_MODEL), F32),
            pltpu.VMEM((M + HALO_B, D_MODEL), F32),
            pltpu.VMEM((2, M, D_MODEL), F32),
            pltpu.VMEM((M, D_MODEL), BF16),
            pltpu.VMEM((M, D_MODEL), BF16),
            pltpu.VMEM((BATCH, D_MODEL), F32),
        ],
        compiler_params=pltpu.CompilerParams(
            dimension_semantics=("arbitrary",), vmem_limit_bytes=VMEM_LIMIT_BYTES),
        name="mixer",
    )(*operands)


def _ffn_call(x1, gffn, wup, w3, b3, wdown, gfin):
    n_rows = x1.shape[0]
    operands = (x1, gffn, wup, w3, b3, wdown, gfin)
    in_specs = [_row_tile()] + [_resident(a.shape) for a in operands[1:]]
    return pl.pallas_call(
        _ffn_kernel,
        grid=(n_rows // M,),
        in_specs=in_specs,
        out_specs=_row_tile(),
        out_shape=jax.ShapeDtypeStruct((n_rows, D_MODEL), F32),
        scratch_shapes=[
            pltpu.VMEM((M, D_MODEL), BF16),
            pltpu.VMEM((M + HALO_F, D_FF), F32),
            pltpu.VMEM((M, D_FF), F32),
            pltpu.VMEM((M, D_FF), BF16),
        ],
        compiler_params=pltpu.CompilerParams(
            dimension_semantics=("arbitrary",), vmem_limit_bytes=VMEM_LIMIT_BYTES),
        name="ffn",
    )(*operands)


def kernel(x, g_mix, w_in, lru_conv_w, lru_conv_b, lru_wa, lru_ba, lru_wx, lru_bx, lru_lambda,
           w_lru_out, conf_dw_w, conf_dw_b, conf_ln_g, conf_ln_b, w_conf_out, b_gate, w_out,
           g_ffn, w_up, ffn_dw_w, ffn_dw_b, w_down, g_final):
    bsz, seq, d = x.shape
    assert (bsz, seq, d) == (BATCH, SEQ, D_MODEL)
    assert g_mix.shape[0] == 1
    row = lambda v: v.reshape(1, -1)
    l = 0

    xt = jnp.transpose(x, (1, 0, 2)).reshape(seq * bsz, d)
    wg = jnp.concatenate(
        [_block_diag_groups(lru_wa[l]), _block_diag_groups(lru_wx[l])], axis=-1).astype(BF16)
    w31 = jnp.repeat(conf_dw_w[l], BATCH, axis=0)
    xt = _mixer_call(
        xt, row(g_mix[l]), w_in[l].astype(BF16), lru_conv_w[l], row(lru_conv_b[l]), wg,
        row(lru_ba[l]), row(lru_bx[l]), row(lru_lambda[l]), w_lru_out[l].astype(BF16),
        w31, row(conf_dw_b[l]), row(conf_ln_g[l]), row(conf_ln_b[l]),
        w_conf_out[l].astype(BF16), row(b_gate[l]), w_out[l].astype(BF16))
    xt = _ffn_call(
        xt, row(g_ffn[l]), w_up[l].astype(BF16), ffn_dw_w[l], row(ffn_dw_b[l]),
        w_down[l].astype(BF16), row(g_final))
    return jnp.transpose(xt.reshape(seq, bsz, d), (1, 0, 2))
```

```python
import jax
import jax.numpy as jnp
from jax import lax
from jax.experimental import pallas as pl
from jax.experimental.pallas import tpu as pltpu

D_MODEL = 1024
BATCH = 8
SEQ = 4096
LRU_HEADS = 16
LRU_HEAD_DIM = D_MODEL // LRU_HEADS
LRU_CONV = 4
LRU_C = 8.0
CONF_KERNEL = 31
D_FF = 3 * D_MODEL
FFN_CONV = 3
EPS = 1e-6

F32 = jnp.float32
BF16 = jnp.bfloat16

LANES = 128
TT = 64
M = TT * BATCH
TBN = 16
SUB = 64
COLB = 256
NB_D = D_MODEL // COLB
NB_F = D_FF // COLB
DOWN_K = 1024
TB31 = 16
HALO_A = (LRU_CONV - 1) * BATCH
HALO_B = (CONF_KERNEL - 1) * BATCH
HALO_F = (FFN_CONV - 1) * BATCH
VMEM_LIMIT_BYTES = 58 * 1024 * 1024
GELU_C0 = 0.7978845608028654
GELU_C1 = 0.044715 * GELU_C0

U_GA, U_CA, U_CB, U_SA, U_SB = range(5)
U_YA, U_YB, U_C31 = U_GA, U_CA, U_CB


def _twice_gelu(z):
    return z + z * jnp.tanh(z * (GELU_C0 + GELU_C1 * (z * z)))


def _trows(i):
    return pl.ds(pl.multiple_of(i * TBN, TBN), TBN)


def _colblock(j):
    return slice(j * COLB, (j + 1) * COLB)


def _time_major_copies(hbm_ref, vmem_ref, sem_ref, step, to_vmem):
    copies = []
    for b in range(BATCH):
        hbm = hbm_ref.at[b, pl.ds(step * TT, TT), :]
        vmem = vmem_ref.at[:, b, :]
        src, dst = (hbm, vmem) if to_vmem else (vmem, hbm)
        copies.append(pltpu.make_async_copy(src, dst, sem_ref.at[b]))
    return copies


def _mixer_kernel(x_hbm, gmix_ref, win_ref, w4_ref, b4_ref, wg_ref, ba_ref, bx_ref, lam_ref,
                  wa_ref, w31_ref, b31_ref, lng_ref, lnb_ref, wb_ref, bgate_ref, wout_ref,
                  o_ref,
                  xbuf, xsem, hb_ref, xcb_ref, xa_ext, u_ref, cg_ext, gp_ref, ya_ref, yb_ref,
                  hcar_ref):
    step = pl.program_id(0)
    slot = step % 2

    def x_copies(s):
        return _time_major_copies(x_hbm, xbuf.at[s % 2], xsem.at[s % 2], s, to_vmem=True)

    @pl.when(step == 0)
    def _():
        for cp in x_copies(0):
            cp.start()
        xa_ext[0:HALO_A, :] = jnp.zeros((HALO_A, D_MODEL), F32)
        cg_ext[0:HALO_B, :] = jnp.zeros((HALO_B, D_MODEL), F32)
        hcar_ref[...] = jnp.zeros((BATCH, D_MODEL), F32)

    @pl.when(step + 1 < pl.num_programs(0))
    def _():
        for cp in x_copies(step + 1):
            cp.start()

    for cp in x_copies(step):
        cp.wait()

    def norm_body(i, c):
        xv = xbuf[slot, _trows(i)].reshape(TBN * BATCH, D_MODEL)
        ms = jnp.mean(xv * xv, axis=-1, keepdims=True)
        rows = pl.ds(pl.multiple_of(i * TBN * BATCH, TBN * BATCH), TBN * BATCH)
        hb_ref[rows, :] = (xv * lax.rsqrt(ms + EPS) * gmix_ref[...]).astype(BF16)
        return c
    lax.fori_loop(0, TT // TBN, norm_body, 0)

    lam = lam_ref[...]
    softplus_neg_lam = jnp.maximum(-lam, 0.0) + jnp.log(1.0 + jnp.exp(-jnp.abs(lam)))
    half_log_a_scale = (-0.5 * LRU_C) * softplus_neg_lam

    def in_proj_block(c):
        cols = _colblock(c)
        hb = hb_ref[...]
        xa_ext[HALO_A:HALO_A + M, cols] = jnp.dot(hb, win_ref[:, cols], preferred_element_type=F32)
        for g in range(5):
            wcols = slice((g + 1) * D_MODEL + c * COLB, (g + 1) * D_MODEL + (c + 1) * COLB)
            u_ref[g, :, cols] = jnp.dot(hb, win_ref[:, wcols], preferred_element_type=F32)

    def conv4_block(c):
        cols = _colblock(c)
        for s in range(M // SUB):
            r = s * SUB
            acc = b4_ref[:, cols] + w4_ref[0:1, cols] * xa_ext[r:r + SUB, cols]
            for k in range(1, LRU_CONV):
                acc = acc + w4_ref[k:k + 1, cols] * xa_ext[r + k * BATCH:r + k * BATCH + SUB, cols]
            xa_ext[r:r + SUB, cols] = acc
            xcb_ref[r:r + SUB, cols] = acc.astype(BF16)

    def gate_block(c):
        cols = _colblock(c)
        gp = jnp.dot(xcb_ref[:, cols], wg_ref[c], preferred_element_type=F32)
        gp_ref[0, :, cols] = gp[:, 0:COLB]
        gp_ref[1, :, cols] = gp[:, COLB:2 * COLB]

    def lru_block(c):
        cols = _colblock(c)
        h = hcar_ref[:, cols]
        hscale = half_log_a_scale[:, cols]
        for s in range(M // SUB):
            rows = slice(s * SUB, (s + 1) * SUB)
            xc = xa_ext[rows, cols]
            tr = jnp.tanh(gp_ref[0, rows, cols] + ba_ref[:, cols])
            ti = jnp.tanh(gp_ref[1, rows, cols] + bx_ref[:, cols])
            log_a = hscale * tr + hscale
            a = jnp.exp(log_a)
            one_minus_a2 = jnp.maximum(jnp.tanh(log_a) * (-1.0 - a * a), 0.0)
            mult = jnp.where(one_minus_a2 > 0.0, one_minus_a2 * lax.rsqrt(one_minus_a2), 0.0)
            u2 = mult * (ti * xc + xc)
            hs = []
            for q in range(SUB // BATCH):
                h = a[q * BATCH:(q + 1) * BATCH, :] * h + u2[q * BATCH:(q + 1) * BATCH, :]
                hs.append(h)
            hh = jnp.concatenate(hs, axis=0)
            ya_ref[rows, cols] = (hh * _twice_gelu(u_ref[U_GA, rows, cols])).astype(BF16)
            ca = u_ref[U_CA, rows, cols]
            cg_ext[HALO_B + s * SUB:HALO_B + (s + 1) * SUB, cols] = (
                ca * jnp.tanh(u_ref[U_CB, rows, cols]) + ca)
        hcar_ref[:, cols] = h

    def conv31_block(c):
        for lc in range(COLB // LANES):
            lanes = slice(c * COLB + lc * LANES, c * COLB + (lc + 1) * LANES)
            taps = [w31_ref[k * BATCH:(k + 1) * BATCH, lanes] for k in range(CONF_KERNEL)]
            bias = b31_ref[:, lanes]
            for tb in range(TT // TB31):
                t0 = tb * TB31
                xs = [cg_ext[(t0 + i) * BATCH:(t0 + i + 1) * BATCH, lanes]
                      for i in range(TB31 + CONF_KERNEL - 1)]
                for t in range(TB31):
                    acc = bias + taps[0] * xs[t]
                    for k in range(1, CONF_KERNEL):
                        acc = acc + taps[k] * xs[t + k]
                    u_ref[U_C31, (t0 + t) * BATCH:(t0 + t + 1) * BATCH, lanes] = acc

    for c in range(NB_D):
        in_proj_block(c)
        conv4_block(c)
        gate_block(c)
        lru_block(c)
        conv31_block(c)

    xa_ext[0:HALO_A, :] = xa_ext[M:M + HALO_A, :]
    cg_ext[0:HALO_B, :] = cg_ext[M:M + HALO_B, :]

    u_ref[U_YA] = jnp.dot(ya_ref[...], wa_ref[...], preferred_element_type=F32)

    for s in range(M // SUB):
        rows = slice(s * SUB, (s + 1) * SUB)
        acc = u_ref[U_C31, rows, :]
        mu = jnp.mean(acc, axis=-1, keepdims=True)
        cen = acc - mu
        var = jnp.mean(cen * cen, axis=-1, keepdims=True)
        y = cen * lax.rsqrt(var + EPS) * lng_ref[...] + lnb_ref[...]
        yb_ref[rows, :] = (y * jnp.tanh(0.5 * y) + y).astype(BF16)

    u_ref[U_YB] = jnp.dot(yb_ref[...], wb_ref[...], preferred_element_type=F32)

    for s in range(M // SUB):
        rows = slice(s * SUB, (s + 1) * SUB)
        ya = u_ref[U_YA, rows, :]
        yb = u_ref[U_YB, rows, :]
        ta = jnp.tanh(u_ref[U_SA, rows, :] + bgate_ref[:, 0:D_MODEL])
        tb = jnp.tanh(u_ref[U_SB, rows, :] + bgate_ref[:, D_MODEL:2 * D_MODEL])
        hb_ref[rows, :] = ((ta * ya + ya) + (tb * yb + yb)).astype(BF16)

    o_ref[...] = (xbuf[slot].reshape(M, D_MODEL)
                  + jnp.dot(hb_ref[...], wout_ref[...], preferred_element_type=F32))


def _ffn_kernel(x_ref, gffn_ref, wup_ref, w3_ref, b3_ref, wdown_ref, gfin_ref,
                o_hbm,
                obuf, osem, acc_ref, hb_ref, g_ext, v_ref, p_ref):
    step = pl.program_id(0)

    def o_copies(s):
        return _time_major_copies(o_hbm, obuf, osem, s, to_vmem=False)

    @pl.when(step == 0)
    def _():
        g_ext[0:HALO_F, :] = jnp.zeros((HALO_F, D_FF), F32)

    def norm_body(i, c):
        rows = pl.ds(pl.multiple_of(i * TBN * BATCH, TBN * BATCH), TBN * BATCH)
        xv = x_ref[rows, :]
        ms = jnp.mean(xv * xv, axis=-1, keepdims=True)
        hb_ref[rows, :] = (xv * lax.rsqrt(ms + EPS) * gffn_ref[...]).astype(BF16)
        acc_ref[rows, :] = xv
        return c
    lax.fori_loop(0, TT // TBN, norm_body, 0)

    def up_block(j):
        cols = _colblock(j)
        vcols = slice(D_FF + j * COLB, D_FF + (j + 1) * COLB)
        hb = hb_ref[...]
        g_ext[HALO_F:HALO_F + M, cols] = jnp.dot(hb, wup_ref[:, cols], preferred_element_type=F32)
        v_ref[:, cols] = jnp.dot(hb, wup_ref[:, vcols], preferred_element_type=F32)

    def act_block(j):
        cols = _colblock(j)
        for s in range(M // SUB):
            r = s * SUB
            acc = b3_ref[:, cols] + w3_ref[0:1, cols] * g_ext[r:r + SUB, cols]
            for k in range(1, FFN_CONV):
                acc = acc + w3_ref[k:k + 1, cols] * g_ext[r + k * BATCH:r + k * BATCH + SUB, cols]
            p_ref[r:r + SUB, cols] = (_twice_gelu(acc) * v_ref[r:r + SUB, cols]).astype(BF16)
        g_ext[0:HALO_F, cols] = g_ext[M:M + HALO_F, cols]

    def down_chunk(c):
        rows = slice(c * DOWN_K, (c + 1) * DOWN_K)
        acc_ref[...] += jnp.dot(p_ref[:, rows], wdown_ref[rows, :], preferred_element_type=F32)

    for j in range(NB_F):
        up_block(j)
        act_block(j)
        if (j + 1) % (DOWN_K // COLB) == 0:
            down_chunk(j // (DOWN_K // COLB))

    @pl.when(step > 0)
    def _():
        for cp in o_copies(step - 1):
            cp.wait()

    def out_body(i, c):
        rows = pl.ds(pl.multiple_of(i * TBN * BATCH, TBN * BATCH), TBN * BATCH)
        xv = acc_ref[rows, :]
        ms = jnp.mean(xv * xv, axis=-1, keepdims=True)
        y = xv * lax.rsqrt(ms + EPS) * gfin_ref[...]
        obuf[_trows(i)] = y.reshape(TBN, BATCH, D_MODEL)
        return c
    lax.fori_loop(0, TT // TBN, out_body, 0)

    for cp in o_copies(step):
        cp.start()

    @pl.when(step + 1 == pl.num_programs(0))
    def _():
        for cp in o_copies(step):
            cp.wait()


def _resident(shape):
    nd = len(shape)
    return pl.BlockSpec(shape, lambda t: (0,) * nd, pipeline_mode=pl.Buffered(1))


def _row_tile():
    return pl.BlockSpec((M, D_MODEL), lambda t: (t, 0))


def _block_diag_groups(w):
    hpg = COLB // LRU_HEAD_DIM
    w4 = w.reshape(NB_D, hpg, LRU_HEAD_DIM, LRU_HEAD_DIM)
    eye = jnp.eye(hpg, dtype=w.dtype)
    return jnp.einsum("ghde,hk->ghdke", w4, eye).reshape(NB_D, COLB, COLB)


def _mixer_call(x, gmix, win, w4, b4, wg, ba, bx, lam, wa, w31, b31, lng, lnb, wb, bgate, wout):
    operands = (x, gmix, win, w4, b4, wg, ba, bx, lam, wa, w31, b31, lng, lnb, wb, bgate, wout)
    in_specs = ([pl.BlockSpec(memory_space=pl.ANY)]
                + [_resident(a.shape) for a in operands[1:]])
    return pl.pallas_call(
        _mixer_kernel,
        grid=(SEQ // TT,),
        in_specs=in_specs,
        out_specs=_row_tile(),
        out_shape=jax.ShapeDtypeStruct((SEQ * BATCH, D_MODEL), F32),
        scratch_shapes=[
            pltpu.VMEM((2, TT, BATCH, D_MODEL), F32),
            pltpu.SemaphoreType.DMA((2, BATCH)),
            pltpu.VMEM((M, D_MODEL), BF16),
            pltpu.VMEM((M, D_MODEL), BF16),
            pltpu.VMEM((M + HALO_A, D_MODEL), F32),
            pltpu.VMEM((5, M, D_MODEL), F32),
            pltpu.VMEM((M + HALO_B, D_MODEL), F32),
            pltpu.VMEM((2, M, D_MODEL), F32),
            pltpu.VMEM((M, D_MODEL), BF16),
            pltpu.VMEM((M, D_MODEL), BF16),
            pltpu.VMEM((BATCH, D_MODEL), F32),
        ],
        compiler_params=pltpu.CompilerParams(
            dimension_semantics=("arbitrary",), vmem_limit_bytes=VMEM_LIMIT_BYTES),
        name="mixer",
    )(*operands)


def _ffn_call(x1, gffn, wup, w3, b3, wdown, gfin):
    operands = (x1, gffn, wup, w3, b3, wdown, gfin)
    in_specs = [_row_tile()] + [_resident(a.shape) for a in operands[1:]]
    return pl.pallas_call(
        _ffn_kernel,
        grid=(SEQ // TT,),
        in_specs=in_specs,
        out_specs=pl.BlockSpec(memory_space=pl.ANY),
        out_shape=jax.ShapeDtypeStruct((BATCH, SEQ, D_MODEL), F32),
        scratch_shapes=[
            pltpu.VMEM((TT, BATCH, D_MODEL), F32),
            pltpu.SemaphoreType.DMA((BATCH,)),
            pltpu.VMEM((M, D_MODEL), F32),
            pltpu.VMEM((M, D_MODEL), BF16),
            pltpu.VMEM((M + HALO_F, D_FF), F32),
            pltpu.VMEM((M, D_FF), F32),
            pltpu.VMEM((M, D_FF), BF16),
        ],
        compiler_params=pltpu.CompilerParams(
            dimension_semantics=("arbitrary",), vmem_limit_bytes=VMEM_LIMIT_BYTES),
        name="ffn",
    )(*operands)


def kernel(x, g_mix, w_in, lru_conv_w, lru_conv_b, lru_wa, lru_ba, lru_wx, lru_bx, lru_lambda,
           w_lru_out, conf_dw_w, conf_dw_b, conf_ln_g, conf_ln_b, w_conf_out, b_gate, w_out,
           g_ffn, w_up, ffn_dw_w, ffn_dw_b, w_down, g_final):
    assert x.shape == (BATCH, SEQ, D_MODEL)
    assert g_mix.shape[0] == 1
    row = lambda v: v.reshape(1, -1)
    l = 0

    ones, half = jnp.ones((D_MODEL,), F32), jnp.full((D_MODEL,), 0.5, F32)
    win = (w_in[l] * jnp.concatenate([ones, ones, ones, half, half, half])).astype(BF16)
    wg = (0.5 * jnp.concatenate(
        [_block_diag_groups(lru_wa[l]), _block_diag_groups(lru_wx[l])], axis=-1)).astype(BF16)
    w31 = jnp.repeat(0.5 * conf_dw_w[l], BATCH, axis=0)
    x1 = _mixer_call(
        x, row(g_mix[l]), win, lru_conv_w[l], row(lru_conv_b[l]), wg,
        row(0.5 * lru_ba[l]), row(0.5 * lru_bx[l]), row(lru_lambda[l]),
        (0.25 * w_lru_out[l]).astype(BF16), w31, row(conf_dw_b[l]), row(conf_ln_g[l]),
        row(conf_ln_b[l]), (0.5 * w_conf_out[l]).astype(BF16), row(0.5 * b_gate[l]),
        (0.5 * w_out[l]).astype(BF16))
    wup = (w_up[l] * jnp.concatenate([jnp.ones((D_FF,), F32), jnp.full((D_FF,), 0.5, F32)])
           ).astype(BF16)
    return _ffn_call(
        x1, row(g_ffn[l]), wup, ffn_dw_w[l], row(ffn_dw_b[l]), w_down[l].astype(BF16),
        row(g_final))
```

```python
import jax
import jax.numpy as jnp
from jax import lax
from jax.experimental import pallas as pl
from jax.experimental.pallas import tpu as pltpu

D_MODEL = 1024
BATCH = 8
SEQ = 4096
LRU_HEADS = 16
LRU_HEAD_DIM = D_MODEL // LRU_HEADS
LRU_CONV = 4
LRU_C = 8.0
CONF_KERNEL = 31
D_FF = 3 * D_MODEL
FFN_CONV = 3
EPS = 1e-6

F32 = jnp.float32
BF16 = jnp.bfloat16

LANES = 128
TT = 64
N_STEPS = SEQ // TT
M = TT * BATCH
TBN = 16
RBN = TBN * BATCH
SUB = 64
COLB = 256
NB_D = D_MODEL // COLB
NB_F = D_FF // COLB
DOWN_K = 1024
TB31 = 16
X_SLOTS = 3
HALO_A = (LRU_CONV - 1) * BATCH
HALO_B = (CONF_KERNEL - 1) * BATCH
HALO_F = (FFN_CONV - 1) * BATCH
VMEM_LIMIT_BYTES = 58 * 1024 * 1024
GELU_C0 = 0.7978845608028654
GELU_C1 = 0.044715 * GELU_C0

U_GA, U_CA, U_CB, U_SA, U_SB = range(5)
U_YA, U_YB, U_C31 = U_GA, U_CA, U_CB


def _twice_gelu(z):
    return z + z * jnp.tanh(z * (GELU_C0 + GELU_C1 * (z * z)))


def _rms_scale(xv, gain_ref):
    ms = jnp.mean(xv * xv, axis=-1, keepdims=True)
    return xv * lax.rsqrt(ms + EPS) * gain_ref[...]


def _colblock(j):
    return slice(j * COLB, (j + 1) * COLB)


def _time_major_copies(hbm_ref, vmem_ref, sem_ref, step, to_vmem):
    copies = []
    for b in range(BATCH):
        hbm = hbm_ref.at[b, pl.ds(step * TT, TT), :]
        vmem = vmem_ref.at[:, b, :]
        src, dst = (hbm, vmem) if to_vmem else (vmem, hbm)
        copies.append(pltpu.make_async_copy(src, dst, sem_ref.at[b]))
    return copies


def _mixer_kernel(x_hbm, gmix_ref, win_ref, w4_ref, b4_ref, wg_ref, ba_ref, bx_ref, lam_ref,
                  wa_ref, w31_ref, b31_ref, lng_ref, lnb_ref, wb_ref, bgate_ref, wout_ref,
                  o_ref,
                  xbuf, xsem, hb_ref, xcb_ref, xa_ext, u_ref, cg_ext, gp_ref, ya_ref, yb_ref,
                  hcar_ref):
    step = pl.program_id(0)
    slot = step % X_SLOTS
    next_slot = (step + 1) % X_SLOTS

    def x_copies(s):
        return _time_major_copies(
            x_hbm, xbuf.at[s % X_SLOTS], xsem.at[s % X_SLOTS], s, to_vmem=True)

    def norm_chunk(src_slot, i):
        xv = xbuf[src_slot, i * TBN:(i + 1) * TBN].reshape(RBN, D_MODEL)
        hb_ref[i * RBN:(i + 1) * RBN, :] = _rms_scale(xv, gmix_ref).astype(BF16)

    @pl.when(step == 0)
    def _():
        for s in range(2):
            for cp in x_copies(s):
                cp.start()
        xa_ext[0:HALO_A, :] = jnp.zeros((HALO_A, D_MODEL), F32)
        cg_ext[0:HALO_B, :] = jnp.zeros((HALO_B, D_MODEL), F32)
        hcar_ref[...] = jnp.zeros((BATCH, D_MODEL), F32)
        for cp in x_copies(0):
            cp.wait()
        for i in range(TT // TBN):
            norm_chunk(0, i)

    @pl.when(step + 2 < N_STEPS)
    def _():
        for cp in x_copies(step + 2):
            cp.start()

    @pl.when(step + 1 < N_STEPS)
    def _():
        for cp in x_copies(step + 1):
            cp.wait()

    lam = lam_ref[...]
    softplus_neg_lam = jnp.maximum(-lam, 0.0) + jnp.log(1.0 + jnp.exp(-jnp.abs(lam)))
    half_log_a_scale = (-0.5 * LRU_C) * softplus_neg_lam

    def in_proj_block(c):
        cols = _colblock(c)
        hb = hb_ref[...]
        xa_ext[HALO_A:HALO_A + M, cols] = jnp.dot(hb, win_ref[:, cols], preferred_element_type=F32)
        for g in range(5):
            wcols = slice((g + 1) * D_MODEL + c * COLB, (g + 1) * D_MODEL + (c + 1) * COLB)
            u_ref[g, :, cols] = jnp.dot(hb, win_ref[:, wcols], preferred_element_type=F32)

    def conv4_block(c):
        cols = _colblock(c)
        for s in range(M // SUB):
            r = s * SUB
            acc = b4_ref[:, cols] + w4_ref[0:1, cols] * xa_ext[r:r + SUB, cols]
            for k in range(1, LRU_CONV):
                acc = acc + w4_ref[k:k + 1, cols] * xa_ext[r + k * BATCH:r + k * BATCH + SUB, cols]
            xa_ext[r:r + SUB, cols] = acc
            xcb_ref[r:r + SUB, cols] = acc.astype(BF16)

    def gate_block(c):
        cols = _colblock(c)
        gp = jnp.dot(xcb_ref[:, cols], wg_ref[c], preferred_element_type=F32)
        gp_ref[0, :, cols] = gp[:, 0:COLB]
        gp_ref[1, :, cols] = gp[:, COLB:2 * COLB]

    def lru_block(c):
        cols = _colblock(c)
        h = hcar_ref[:, cols]
        hscale = half_log_a_scale[:, cols]
        for s in range(M // SUB):
            rows = slice(s * SUB, (s + 1) * SUB)
            xc = xa_ext[rows, cols]
            tr = jnp.tanh(gp_ref[0, rows, cols] + ba_ref[:, cols])
            ti = jnp.tanh(gp_ref[1, rows, cols] + bx_ref[:, cols])
            log_a = hscale * tr + hscale
            a = jnp.exp(log_a)
            one_minus_a2 = jnp.maximum(jnp.tanh(log_a) * (-1.0 - a * a), 0.0)
            mult = jnp.where(one_minus_a2 > 0.0, one_minus_a2 * lax.rsqrt(one_minus_a2), 0.0)
            u2 = mult * (ti * xc + xc)
            hs = []
            for q in range(SUB // BATCH):
                h = a[q * BATCH:(q + 1) * BATCH, :] * h + u2[q * BATCH:(q + 1) * BATCH, :]
                hs.append(h)
            hh = jnp.concatenate(hs, axis=0)
            ya_ref[rows, cols] = (hh * _twice_gelu(u_ref[U_GA, rows, cols])).astype(BF16)
            ca = u_ref[U_CA, rows, cols]
            cg_ext[HALO_B + s * SUB:HALO_B + (s + 1) * SUB, cols] = (
                ca * jnp.tanh(u_ref[U_CB, rows, cols]) + ca)
        hcar_ref[:, cols] = h

    def conv31_block(c):
        for lc in range(COLB // LANES):
            lanes = slice(c * COLB + lc * LANES, c * COLB + (lc + 1) * LANES)
            taps = [w31_ref[k * BATCH:(k + 1) * BATCH, lanes] for k in range(CONF_KERNEL)]
            bias = b31_ref[:, lanes]
            for tb in range(TT // TB31):
                t0 = tb * TB31
                xs = [cg_ext[(t0 + i) * BATCH:(t0 + i + 1) * BATCH, lanes]
                      for i in range(TB31 + CONF_KERNEL - 1)]
                for t in range(TB31):
                    acc = bias + taps[0] * xs[t]
                    for k in range(1, CONF_KERNEL):
                        acc = acc + taps[k] * xs[t + k]
                    u_ref[U_C31, (t0 + t) * BATCH:(t0 + t + 1) * BATCH, lanes] = acc

    for c in range(NB_D):
        in_proj_block(c)
        conv4_block(c)
        gate_block(c)
        lru_block(c)
        conv31_block(c)

    xa_ext[0:HALO_A, :] = xa_ext[M:M + HALO_A, :]
    cg_ext[0:HALO_B, :] = cg_ext[M:M + HALO_B, :]

    u_ref[U_YA] = jnp.dot(ya_ref[...], wa_ref[...], preferred_element_type=F32)

    for s in range(M // SUB):
        rows = slice(s * SUB, (s + 1) * SUB)
        acc = u_ref[U_C31, rows, :]
        mu = jnp.mean(acc, axis=-1, keepdims=True)
        cen = acc - mu
        var = jnp.mean(cen * cen, axis=-1, keepdims=True)
        y = cen * lax.rsqrt(var + EPS) * lng_ref[...] + lnb_ref[...]
        yb_ref[rows, :] = (y * jnp.tanh(0.5 * y) + y).astype(BF16)

    u_ref[U_YB] = jnp.dot(yb_ref[...], wb_ref[...], preferred_element_type=F32)

    for s in range(M // SUB):
        rows = slice(s * SUB, (s + 1) * SUB)
        ya = u_ref[U_YA, rows, :]
        yb = u_ref[U_YB, rows, :]
        ta = jnp.tanh(u_ref[U_SA, rows, :] + bgate_ref[:, 0:D_MODEL])
        tb = jnp.tanh(u_ref[U_SB, rows, :] + bgate_ref[:, D_MODEL:2 * D_MODEL])
        ya_ref[rows, :] = ((ta * ya + ya) + (tb * yb + yb)).astype(BF16)

    o_ref[...] = (xbuf[slot].reshape(M, D_MODEL)
                  + jnp.dot(ya_ref[...], wout_ref[...], preferred_element_type=F32))

    for i in range(TT // TBN):
        norm_chunk(next_slot, i)


def _ffn_kernel(x_ref, xnext_ref, gffn_ref, wup_ref, w3_ref, b3_ref, wdown_ref, gfin_ref,
                o_hbm,
                obuf, osem, acc_ref, hb_ref, hbn_ref, g_ext, v_ref, p_ref):
    step = pl.program_id(0)

    def o_copies(s):
        return _time_major_copies(o_hbm, obuf, osem, s, to_vmem=False)

    @pl.when(step == 0)
    def _():
        g_ext[0:HALO_F, :] = jnp.zeros((HALO_F, D_FF), F32)
        for i in range(M // RBN):
            rows = slice(i * RBN, (i + 1) * RBN)
            hb_ref[rows, :] = _rms_scale(x_ref[rows, :], gffn_ref).astype(BF16)

    def up_block(j):
        cols = _colblock(j)
        vcols = slice(D_FF + j * COLB, D_FF + (j + 1) * COLB)
        hb = hb_ref[...]
        g_ext[HALO_F:HALO_F + M, cols] = jnp.dot(hb, wup_ref[:, cols], preferred_element_type=F32)
        v_ref[:, cols] = jnp.dot(hb, wup_ref[:, vcols], preferred_element_type=F32)

    def act_block(j):
        cols = _colblock(j)
        for s in range(M // SUB):
            r = s * SUB
            acc = b3_ref[:, cols] + w3_ref[0:1, cols] * g_ext[r:r + SUB, cols]
            for k in range(1, FFN_CONV):
                acc = acc + w3_ref[k:k + 1, cols] * g_ext[r + k * BATCH:r + k * BATCH + SUB, cols]
            p_ref[r:r + SUB, cols] = (_twice_gelu(acc) * v_ref[r:r + SUB, cols]).astype(BF16)
        g_ext[0:HALO_F, cols] = g_ext[M:M + HALO_F, cols]

    def down_chunk(c):
        rows = slice(c * DOWN_K, (c + 1) * DOWN_K)
        part = jnp.dot(p_ref[:, rows], wdown_ref[rows, :], preferred_element_type=F32)
        acc_ref[...] = (x_ref[...] if c == 0 else acc_ref[...]) + part

    for j in range(NB_F):
        up_block(j)
        act_block(j)
        if (j + 1) % (DOWN_K // COLB) == 0:
            down_chunk(j // (DOWN_K // COLB))
        if j < M // RBN:
            rows = slice(j * RBN, (j + 1) * RBN)
            hbn_ref[rows, :] = _rms_scale(xnext_ref[rows, :], gffn_ref).astype(BF16)

    @pl.when(step > 0)
    def _():
        for cp in o_copies(step - 1):
            cp.wait()

    def out_body(i, c):
        rows = pl.ds(pl.multiple_of(i * RBN, RBN), RBN)
        y = _rms_scale(acc_ref[rows, :], gfin_ref)
        obuf[pl.ds(pl.multiple_of(i * TBN, TBN), TBN)] = y.reshape(TBN, BATCH, D_MODEL)
        return c
    lax.fori_loop(0, TT // TBN, out_body, 0)

    for cp in o_copies(step):
        cp.start()

    hb_ref[...] = hbn_ref[...]

    @pl.when(step + 1 == N_STEPS)
    def _():
        for cp in o_copies(step):
            cp.wait()


def _resident(shape):
    nd = len(shape)
    return pl.BlockSpec(shape, lambda t: (0,) * nd, pipeline_mode=pl.Buffered(1))


def _row_tile(ahead=0):
    return pl.BlockSpec((M, D_MODEL), lambda t: (jnp.minimum(t + ahead, N_STEPS - 1), 0))


def _block_diag_groups(w):
    hpg = COLB // LRU_HEAD_DIM
    w4 = w.reshape(NB_D, hpg, LRU_HEAD_DIM, LRU_HEAD_DIM)
    eye = jnp.eye(hpg, dtype=w.dtype)
    return jnp.einsum("ghde,hk->ghdke", w4, eye).reshape(NB_D, COLB, COLB)


def _mixer_call(x, gmix, win, w4, b4, wg, ba, bx, lam, wa, w31, b31, lng, lnb, wb, bgate, wout):
    operands = (x, gmix, win, w4, b4, wg, ba, bx, lam, wa, w31, b31, lng, lnb, wb, bgate, wout)
    in_specs = ([pl.BlockSpec(memory_space=pl.ANY)]
                + [_resident(a.shape) for a in operands[1:]])
    return pl.pallas_call(
        _mixer_kernel,
        grid=(N_STEPS,),
        in_specs=in_specs,
        out_specs=_row_tile(),
        out_shape=jax.ShapeDtypeStruct((SEQ * BATCH, D_MODEL), F32),
        scratch_shapes=[
            pltpu.VMEM((X_SLOTS, TT, BATCH, D_MODEL), F32),
            pltpu.SemaphoreType.DMA((X_SLOTS, BATCH)),
            pltpu.VMEM((M, D_MODEL), BF16),
            pltpu.VMEM((M, D_MODEL), BF16),
            pltpu.VMEM((M + HALO_A, D_MODEL), F32),
            pltpu.VMEM((5, M, D_MODEL), F32),
            pltpu.VMEM((M + HALO_B, D_MODEL), F32),
            pltpu.VMEM((2, M, D_MODEL), F32),
            pltpu.VMEM((M, D_MODEL), BF16),
            pltpu.VMEM((M, D_MODEL), BF16),
            pltpu.VMEM((BATCH, D_MODEL), F32),
        ],
        compiler_params=pltpu.CompilerParams(
            dimension_semantics=("arbitrary",), vmem_limit_bytes=VMEM_LIMIT_BYTES),
        name="mixer",
    )(*operands)


def _ffn_call(x1, gffn, wup, w3, b3, wdown, gfin):
    operands = (x1, x1, gffn, wup, w3, b3, wdown, gfin)
    in_specs = [_row_tile(), _row_tile(ahead=1)] + [_resident(a.shape) for a in operands[2:]]
    return pl.pallas_call(
        _ffn_kernel,
        grid=(N_STEPS,),
        in_specs=in_specs,
        out_specs=pl.BlockSpec(memory_space=pl.ANY),
        out_shape=jax.ShapeDtypeStruct((BATCH, SEQ, D_MODEL), F32),
        scratch_shapes=[
            pltpu.VMEM((TT, BATCH, D_MODEL), F32),
            pltpu.SemaphoreType.DMA((BATCH,)),
            pltpu.VMEM((M, D_MODEL), F32),
            pltpu.VMEM((M, D_MODEL), BF16),
            pltpu.VMEM((M, D_MODEL), BF16),
            pltpu.VMEM((M + HALO_F, D_FF), F32),
            pltpu.VMEM((M, D_FF), F32),
            pltpu.VMEM((M, D_FF), BF16),
        ],
        compiler_params=pltpu.CompilerParams(
            dimension_semantics=("arbitrary",), vmem_limit_bytes=VMEM_LIMIT_BYTES),
        name="ffn",
    )(*operands)


def kernel(x, g_mix, w_in, lru_conv_w, lru_conv_b, lru_wa, lru_ba, lru_wx, lru_bx, lru_lambda,
           w_lru_out, conf_dw_w, conf_dw_b, conf_ln_g, conf_ln_b, w_conf_out, b_gate, w_out,
           g_ffn, w_up, ffn_dw_w, ffn_dw_b, w_down, g_final):
    assert x.shape == (BATCH, SEQ, D_MODEL)
    assert g_mix.shape[0] == 1
    row = lambda v: v.reshape(1, -1)
    l = 0

    ones, half = jnp.ones((D_MODEL,), F32), jnp.full((D_MODEL,), 0.5, F32)
    win = (w_in[l] * jnp.concatenate([ones, ones, ones, half, half, half])).astype(BF16)
    wg = (0.5 * jnp.concatenate(
        [_block_diag_groups(lru_wa[l]), _block_diag_groups(lru_wx[l])], axis=-1)).astype(BF16)
    w31 = jnp.repeat(0.5 * conf_dw_w[l], BATCH, axis=0)
    x1 = _mixer_call(
        x, row(g_mix[l]), win, lru_conv_w[l], row(lru_conv_b[l]), wg,
        row(0.5 * lru_ba[l]), row(0.5 * lru_bx[l]), row(lru_lambda[l]),
        (0.25 * w_lru_out[l]).astype(BF16), w31, row(conf_dw_b[l]), row(conf_ln_g[l]),
        row(conf_ln_b[l]), (0.5 * w_conf_out[l]).astype(BF16), row(0.5 * b_gate[l]),
        (0.5 * w_out[l]).astype(BF16))
    wup = (w_up[l] * jnp.concatenate([jnp.ones((D_FF,), F32), jnp.full((D_FF,), 0.5, F32)])
           ).astype(BF16)
    return _ffn_call(
        x1, row(g_ffn[l]), wup, ffn_dw_w[l], row(ffn_dw_b[l]), w_down[l].astype(BF16),
        row(g_final))
```

```python
import jax
import jax.numpy as jnp
from jax import lax
from jax.experimental import pallas as pl
from jax.experimental.pallas import tpu as pltpu

D_MODEL = 1024
BATCH = 8
SEQ = 4096
LRU_HEADS = 16
LRU_HEAD_DIM = D_MODEL // LRU_HEADS
LRU_CONV = 4
LRU_C = 8.0
CONF_KERNEL = 31
D_FF = 3 * D_MODEL
FFN_CONV = 3
EPS = 1e-6

F32 = jnp.float32
BF16 = jnp.bfloat16

LANES = 128
TT = 64
N_STEPS = SEQ // TT
M = TT * BATCH
TBN = 16
RBN = TBN * BATCH
SUB = 128
COLB = 256
NB_D = D_MODEL // COLB
NB_F = D_FF // COLB
DOWN_K = 1024
TB31 = 8
X_SLOTS = 3
HALO_A = (LRU_CONV - 1) * BATCH
HALO_B = (CONF_KERNEL - 1) * BATCH
HALO_F = (FFN_CONV - 1) * BATCH
VMEM_LIMIT_BYTES = 58 * 1024 * 1024
GELU_C0 = 0.7978845608028654
GELU_C1 = 0.044715 * GELU_C0

U_GA, U_CA, U_CB, U_SA, U_SB = range(5)
U_YA, U_YB, U_C31 = U_GA, U_CA, U_CB


def _twice_gelu(z):
    return z + z * jnp.tanh(z * (GELU_C0 + GELU_C1 * (z * z)))


def _rms_scale(xv, gain_ref):
    ms = jnp.mean(xv * xv, axis=-1, keepdims=True)
    return xv * lax.rsqrt(ms + EPS) * gain_ref[...]


def _colblock(j):
    return slice(j * COLB, (j + 1) * COLB)


def _time_major_copies(hbm_ref, vmem_ref, sem_ref, step, to_vmem):
    copies = []
    for b in range(BATCH):
        hbm = hbm_ref.at[b, pl.ds(step * TT, TT), :]
        vmem = vmem_ref.at[:, b, :]
        src, dst = (hbm, vmem) if to_vmem else (vmem, hbm)
        copies.append(pltpu.make_async_copy(src, dst, sem_ref.at[b]))
    return copies


def _mixer_kernel(x_hbm, gmix_ref, win_ref, w4_ref, b4_ref, wg_ref, ba_ref, bx_ref, lam_ref,
                  wa_ref, w31_ref, b31_ref, lng_ref, lnb_ref, wb_ref, bgate_ref, wout_ref,
                  o_ref,
                  xbuf, xsem, hb_ref, xcb_ref, xa_ext, u_ref, cg_ext, gp_ref, ya_ref, yb_ref,
                  hcar_ref):
    step = pl.program_id(0)
    slot = step % X_SLOTS
    next_slot = (step + 1) % X_SLOTS

    def x_copies(s):
        return _time_major_copies(
            x_hbm, xbuf.at[s % X_SLOTS], xsem.at[s % X_SLOTS], s, to_vmem=True)

    def norm_chunk(src_slot, i):
        xv = xbuf[src_slot, i * TBN:(i + 1) * TBN].reshape(RBN, D_MODEL)
        hb_ref[i * RBN:(i + 1) * RBN, :] = _rms_scale(xv, gmix_ref).astype(BF16)

    @pl.when(step == 0)
    def _():
        for s in range(2):
            for cp in x_copies(s):
                cp.start()
        xa_ext[0:HALO_A, :] = jnp.zeros((HALO_A, D_MODEL), F32)
        cg_ext[0:HALO_B, :] = jnp.zeros((HALO_B, D_MODEL), F32)
        hcar_ref[...] = jnp.zeros((BATCH, D_MODEL), F32)
        for cp in x_copies(0):
            cp.wait()
        for i in range(TT // TBN):
            norm_chunk(0, i)

    @pl.when(step + 2 < N_STEPS)
    def _():
        for cp in x_copies(step + 2):
            cp.start()

    @pl.when(step + 1 < N_STEPS)
    def _():
        for cp in x_copies(step + 1):
            cp.wait()

    lam = lam_ref[...]
    softplus_neg_lam = jnp.maximum(-lam, 0.0) + jnp.log(1.0 + jnp.exp(-jnp.abs(lam)))
    half_log_a_scale = (-0.5 * LRU_C) * softplus_neg_lam

    def in_proj_block(c):
        cols = _colblock(c)
        hb = hb_ref[...]
        xa_ext[HALO_A:HALO_A + M, cols] = jnp.dot(hb, win_ref[:, cols], preferred_element_type=F32)
        for g in range(5):
            wcols = slice((g + 1) * D_MODEL + c * COLB, (g + 1) * D_MODEL + (c + 1) * COLB)
            u_ref[g, :, cols] = jnp.dot(hb, win_ref[:, wcols], preferred_element_type=F32)

    def conv4_block(c):
        cols = _colblock(c)
        for s in range(M // SUB):
            r = s * SUB
            acc = b4_ref[:, cols] + w4_ref[0:1, cols] * xa_ext[r:r + SUB, cols]
            for k in range(1, LRU_CONV):
                acc = acc + w4_ref[k:k + 1, cols] * xa_ext[r + k * BATCH:r + k * BATCH + SUB, cols]
            xa_ext[r:r + SUB, cols] = acc
            xcb_ref[r:r + SUB, cols] = acc.astype(BF16)

    def gate_block(c):
        cols = _colblock(c)
        gp = jnp.dot(xcb_ref[:, cols], wg_ref[c], preferred_element_type=F32)
        gp_ref[0, :, cols] = gp[:, 0:COLB]
        gp_ref[1, :, cols] = gp[:, COLB:2 * COLB]

    def lru_block(c):
        cols = _colblock(c)
        h = hcar_ref[:, cols]
        hscale = half_log_a_scale[:, cols]
        for s in range(M // SUB):
            rows = slice(s * SUB, (s + 1) * SUB)
            xc = xa_ext[rows, cols]
            tr = jnp.tanh(gp_ref[0, rows, cols] + ba_ref[:, cols])
            ti = jnp.tanh(gp_ref[1, rows, cols] + bx_ref[:, cols])
            log_a = hscale * tr + hscale
            a = jnp.exp(log_a)
            one_minus_a2 = jnp.maximum(jnp.tanh(log_a) * (-1.0 - a * a), 0.0)
            mult = jnp.where(one_minus_a2 > 0.0, one_minus_a2 * lax.rsqrt(one_minus_a2), 0.0)
            u2 = mult * (ti * xc + xc)
            hs = []
            for q in range(SUB // BATCH):
                h = a[q * BATCH:(q + 1) * BATCH, :] * h + u2[q * BATCH:(q + 1) * BATCH, :]
                hs.append(h)
            hh = jnp.concatenate(hs, axis=0)
            ya_ref[rows, cols] = (hh * _twice_gelu(u_ref[U_GA, rows, cols])).astype(BF16)
            ca = u_ref[U_CA, rows, cols]
            cg_ext[HALO_B + s * SUB:HALO_B + (s + 1) * SUB, cols] = (
                ca * jnp.tanh(u_ref[U_CB, rows, cols]) + ca)
        hcar_ref[:, cols] = h

    def conv31_block(c):
        for lc in range(COLB // LANES):
            lanes = slice(c * COLB + lc * LANES, c * COLB + (lc + 1) * LANES)
            taps = [w31_ref[k * BATCH:(k + 1) * BATCH, lanes] for k in range(CONF_KERNEL)]
            bias = b31_ref[:, lanes]
            for tb in range(TT // TB31):
                t0 = tb * TB31
                xs = [cg_ext[(t0 + i) * BATCH:(t0 + i + 1) * BATCH, lanes]
                      for i in range(TB31 + CONF_KERNEL - 1)]
                for t in range(TB31):
                    acc = bias + taps[0] * xs[t]
                    for k in range(1, CONF_KERNEL):
                        acc = acc + taps[k] * xs[t + k]
                    u_ref[U_C31, (t0 + t) * BATCH:(t0 + t + 1) * BATCH, lanes] = acc

    for c in range(NB_D):
        in_proj_block(c)
        conv4_block(c)
        gate_block(c)
        lru_block(c)
        conv31_block(c)

    xa_ext[0:HALO_A, :] = xa_ext[M:M + HALO_A, :]
    cg_ext[0:HALO_B, :] = cg_ext[M:M + HALO_B, :]

    u_ref[U_YA] = jnp.dot(ya_ref[...], wa_ref[...], preferred_element_type=F32)

    for s in range(M // SUB):
        rows = slice(s * SUB, (s + 1) * SUB)
        acc = u_ref[U_C31, rows, :]
        mu = jnp.mean(acc, axis=-1, keepdims=True)
        cen = acc - mu
        var = jnp.mean(cen * cen, axis=-1, keepdims=True)
        y = cen * lax.rsqrt(var + EPS) * lng_ref[...] + lnb_ref[...]
        yb_ref[rows, :] = (y * jnp.tanh(0.5 * y) + y).astype(BF16)

    u_ref[U_YB] = jnp.dot(yb_ref[...], wb_ref[...], preferred_element_type=F32)

    for s in range(M // SUB):
        rows = slice(s * SUB, (s + 1) * SUB)
        ya = u_ref[U_YA, rows, :]
        yb = u_ref[U_YB, rows, :]
        ta = jnp.tanh(u_ref[U_SA, rows, :] + bgate_ref[:, 0:D_MODEL])
        tb = jnp.tanh(u_ref[U_SB, rows, :] + bgate_ref[:, D_MODEL:2 * D_MODEL])
        ya_ref[rows, :] = ((ta * ya + ya) + (tb * yb + yb)).astype(BF16)

    o_ref[...] = (xbuf[slot].reshape(M, D_MODEL)
                  + jnp.dot(ya_ref[...], wout_ref[...], preferred_element_type=F32))

    for i in range(TT // TBN):
        norm_chunk(next_slot, i)


def _ffn_kernel(x_ref, xnext_ref, gffn_ref, wup_ref, w3_ref, b3_ref, wdown_ref, gfin_ref,
                o_hbm,
                obuf, osem, acc_ref, hb_ref, hbn_ref, g_ext, v_ref, p_ref):
    step = pl.program_id(0)

    def o_copies(s):
        return _time_major_copies(o_hbm, obuf, osem, s, to_vmem=False)

    @pl.when(step == 0)
    def _():
        g_ext[0:HALO_F, :] = jnp.zeros((HALO_F, D_FF), F32)
        for i in range(M // RBN):
            rows = slice(i * RBN, (i + 1) * RBN)
            hb_ref[rows, :] = _rms_scale(x_ref[rows, :], gffn_ref).astype(BF16)

    def up_block(j):
        cols = _colblock(j)
        vcols = slice(D_FF + j * COLB, D_FF + (j + 1) * COLB)
        hb = hb_ref[...]
        g_ext[HALO_F:HALO_F + M, cols] = jnp.dot(hb, wup_ref[:, cols], preferred_element_type=F32)
        v_ref[:, cols] = jnp.dot(hb, wup_ref[:, vcols], preferred_element_type=F32)

    def act_block(j):
        cols = _colblock(j)
        for s in range(M // SUB):
            r = s * SUB
            acc = b3_ref[:, cols] + w3_ref[0:1, cols] * g_ext[r:r + SUB, cols]
            for k in range(1, FFN_CONV):
                acc = acc + w3_ref[k:k + 1, cols] * g_ext[r + k * BATCH:r + k * BATCH + SUB, cols]
            p_ref[r:r + SUB, cols] = (_twice_gelu(acc) * v_ref[r:r + SUB, cols]).astype(BF16)
        g_ext[0:HALO_F, cols] = g_ext[M:M + HALO_F, cols]

    def down_chunk(c):
        rows = slice(c * DOWN_K, (c + 1) * DOWN_K)
        part = jnp.dot(p_ref[:, rows], wdown_ref[rows, :], preferred_element_type=F32)
        acc_ref[...] = (x_ref[...] if c == 0 else acc_ref[...]) + part

    for j in range(NB_F):
        up_block(j)
        act_block(j)
        if (j + 1) % (DOWN_K // COLB) == 0:
            down_chunk(j // (DOWN_K // COLB))
        if j < M // RBN:
            rows = slice(j * RBN, (j + 1) * RBN)
            hbn_ref[rows, :] = _rms_scale(xnext_ref[rows, :], gffn_ref).astype(BF16)

    @pl.when(step > 0)
    def _():
        for cp in o_copies(step - 1):
            cp.wait()

    def out_body(i, c):
        rows = pl.ds(pl.multiple_of(i * RBN, RBN), RBN)
        y = _rms_scale(acc_ref[rows, :], gfin_ref)
        obuf[pl.ds(pl.multiple_of(i * TBN, TBN), TBN)] = y.reshape(TBN, BATCH, D_MODEL)
        return c
    lax.fori_loop(0, TT // TBN, out_body, 0)

    for cp in o_copies(step):
        cp.start()

    hb_ref[...] = hbn_ref[...]

    @pl.when(step + 1 == N_STEPS)
    def _():
        for cp in o_copies(step):
            cp.wait()


def _resident(shape):
    nd = len(shape)
    return pl.BlockSpec(shape, lambda t: (0,) * nd, pipeline_mode=pl.Buffered(1))


def _row_tile(ahead=0):
    return pl.BlockSpec((M, D_MODEL), lambda t: (jnp.minimum(t + ahead, N_STEPS - 1), 0))


def _block_diag_groups(w):
    hpg = COLB // LRU_HEAD_DIM
    w4 = w.reshape(NB_D, hpg, LRU_HEAD_DIM, LRU_HEAD_DIM)
    eye = jnp.eye(hpg, dtype=w.dtype)
    return jnp.einsum("ghde,hk->ghdke", w4, eye).reshape(NB_D, COLB, COLB)


def _mixer_call(x, gmix, win, w4, b4, wg, ba, bx, lam, wa, w31, b31, lng, lnb, wb, bgate, wout):
    operands = (x, gmix, win, w4, b4, wg, ba, bx, lam, wa, w31, b31, lng, lnb, wb, bgate, wout)
    in_specs = ([pl.BlockSpec(memory_space=pl.ANY)]
                + [_resident(a.shape) for a in operands[1:]])
    return pl.pallas_call(
        _mixer_kernel,
        grid=(N_STEPS,),
        in_specs=in_specs,
        out_specs=_row_tile(),
        out_shape=jax.ShapeDtypeStruct((SEQ * BATCH, D_MODEL), F32),
        scratch_shapes=[
            pltpu.VMEM((X_SLOTS, TT, BATCH, D_MODEL), F32),
            pltpu.SemaphoreType.DMA((X_SLOTS, BATCH)),
            pltpu.VMEM((M, D_MODEL), BF16),
            pltpu.VMEM((M, D_MODEL), BF16),
            pltpu.VMEM((M + HALO_A, D_MODEL), F32),
            pltpu.VMEM((5, M, D_MODEL), F32),
            pltpu.VMEM((M + HALO_B, D_MODEL), F32),
            pltpu.VMEM((2, M, D_MODEL), F32),
            pltpu.VMEM((M, D_MODEL), BF16),
            pltpu.VMEM((M, D_MODEL), BF16),
            pltpu.VMEM((BATCH, D_MODEL), F32),
        ],
        compiler_params=pltpu.CompilerParams(
            dimension_semantics=("arbitrary",), vmem_limit_bytes=VMEM_LIMIT_BYTES),
        name="mixer",
    )(*operands)


def _ffn_call(x1, gffn, wup, w3, b3, wdown, gfin):
    operands = (x1, x1, gffn, wup, w3, b3, wdown, gfin)
    in_specs = [_row_tile(), _row_tile(ahead=1)] + [_resident(a.shape) for a in operands[2:]]
    return pl.pallas_call(
        _ffn_kernel,
        grid=(N_STEPS,),
        in_specs=in_specs,
        out_specs=pl.BlockSpec(memory_space=pl.ANY),
        out_shape=jax.ShapeDtypeStruct((BATCH, SEQ, D_MODEL), F32),
        scratch_shapes=[
            pltpu.VMEM((TT, BATCH, D_MODEL), F32),
            pltpu.SemaphoreType.DMA((BATCH,)),
            pltpu.VMEM((M, D_MODEL), F32),
            pltpu.VMEM((M, D_MODEL), BF16),
            pltpu.VMEM((M, D_MODEL), BF16),
            pltpu.VMEM((M + HALO_F, D_FF), F32),
            pltpu.VMEM((M, D_FF), F32),
            pltpu.VMEM((M, D_FF), BF16),
        ],
        compiler_params=pltpu.CompilerParams(
            dimension_semantics=("arbitrary",), vmem_limit_bytes=VMEM_LIMIT_BYTES),
        name="ffn",
    )(*operands)


def kernel(x, g_mix, w_in, lru_conv_w, lru_conv_b, lru_wa, lru_ba, lru_wx, lru_bx, lru_lambda,
           w_lru_out, conf_dw_w, conf_dw_b, conf_ln_g, conf_ln_b, w_conf_out, b_gate, w_out,
           g_ffn, w_up, ffn_dw_w, ffn_dw_b, w_down, g_final):
    assert x.shape == (BATCH, SEQ, D_MODEL)
    assert g_mix.shape[0] == 1
    row = lambda v: v.reshape(1, -1)
    l = 0

    ones, half = jnp.ones((D_MODEL,), F32), jnp.full((D_MODEL,), 0.5, F32)
    win = (w_in[l] * jnp.concatenate([ones, ones, ones, half, half, half])).astype(BF16)
    wg = (0.5 * jnp.concatenate(
        [_block_diag_groups(lru_wa[l]), _block_diag_groups(lru_wx[l])], axis=-1)).astype(BF16)
    w31 = jnp.repeat(0.5 * conf_dw_w[l], BATCH, axis=0)
    x1 = _mixer_call(
        x, row(g_mix[l]), win, lru_conv_w[l], row(lru_conv_b[l]), wg,
        row(0.5 * lru_ba[l]), row(0.5 * lru_bx[l]), row(lru_lambda[l]),
        (0.25 * w_lru_out[l]).astype(BF16), w31, row(conf_dw_b[l]), row(conf_ln_g[l]),
        row(conf_ln_b[l]), (0.5 * w_conf_out[l]).astype(BF16), row(0.5 * b_gate[l]),
        (0.5 * w_out[l]).astype(BF16))
    wup = (w_up[l] * jnp.concatenate([jnp.ones((D_FF,), F32), jnp.full((D_FF,), 0.5, F32)])
           ).astype(BF16)
    return _ffn_call(
        x1, row(g_ffn[l]), wup, ffn_dw_w[l], row(ffn_dw_b[l]), w_down[l].astype(BF16),
        row(g_final))
```

```python
import jax
import jax.numpy as jnp
from jax import lax
from jax.experimental import pallas as pl
from jax.experimental.pallas import tpu as pltpu

D_MODEL = 1024
BATCH = 8
SEQ = 4096
LRU_HEADS = 16
LRU_HEAD_DIM = D_MODEL // LRU_HEADS
LRU_CONV = 4
LRU_C = 8.0
CONF_KERNEL = 31
D_FF = 3 * D_MODEL
FFN_CONV = 3
EPS = 1e-6

F32 = jnp.float32
BF16 = jnp.bfloat16

LANES = 128
TT = 64
N_STEPS = SEQ // TT
M = TT * BATCH
TBN = 16
RBN = TBN * BATCH
SUB = 128
COLB = 256
NB_D = D_MODEL // COLB
NB_F = D_FF // COLB
DOWN_K = 1024
TB31 = 8
X_SLOTS = 3
W_CHUNK_ROWS = 1024
W_CHUNK_COLS = 512
HALO_A = (LRU_CONV - 1) * BATCH
HALO_B = (CONF_KERNEL - 1) * BATCH
HALO_F = (FFN_CONV - 1) * BATCH
VMEM_LIMIT_BYTES = 58 * 1024 * 1024
GELU_C0 = 0.7978845608028654
GELU_C1 = 0.044715 * GELU_C0

U_GA, U_CA, U_CB, U_SA, U_SB = range(5)
U_YA, U_YB, U_C31 = U_GA, U_CA, U_CB


def _twice_gelu(z):
    return z + z * jnp.tanh(z * (GELU_C0 + GELU_C1 * (z * z)))


def _rms_scale(xv, gain_ref):
    ms = jnp.mean(xv * xv, axis=-1, keepdims=True)
    return xv * lax.rsqrt(ms + EPS) * gain_ref[...]


def _colblock(j):
    return slice(j * COLB, (j + 1) * COLB)


def _time_major_copies(hbm_ref, vmem_ref, sem_ref, step, to_vmem):
    copies = []
    for b in range(BATCH):
        hbm = hbm_ref.at[b, pl.ds(step * TT, TT), :]
        vmem = vmem_ref.at[:, b, :]
        src, dst = (hbm, vmem) if to_vmem else (vmem, hbm)
        copies.append(pltpu.make_async_copy(src, dst, sem_ref.at[b]))
    return copies


def _stream_cast_weights(jobs, stage, sem):
    chunks = []
    for src, dst, column_scale in jobs:
        n_rows, n_cols = dst.shape
        for r0 in range(0, n_rows, W_CHUNK_ROWS):
            for c0 in range(0, n_cols, W_CHUNK_COLS):
                chunks.append((src, dst, r0, c0, column_scale(c0)))

    def copy(i):
        src, _, r0, c0, _ = chunks[i]
        return pltpu.make_async_copy(
            src.at[pl.ds(r0, W_CHUNK_ROWS), pl.ds(c0, W_CHUNK_COLS)], stage.at[i % 2], sem.at[i % 2])

    copy(0).start()
    for i, (_, dst, r0, c0, scale) in enumerate(chunks):
        if i + 1 < len(chunks):
            copy(i + 1).start()
        copy(i).wait()
        w = stage[i % 2]
        if scale != 1.0:
            w = w * scale
        dst[r0:r0 + W_CHUNK_ROWS, c0:c0 + W_CHUNK_COLS] = w.astype(BF16)


def _mixer_kernel(x_hbm, gmix_ref, win_hbm, w4_ref, b4_ref, wg_ref, ba_ref, bx_ref, lam_ref,
                  wa_hbm, w31_ref, b31_ref, lng_ref, lnb_ref, wb_hbm, bgate_ref, wout_hbm,
                  o_ref,
                  xbuf, xsem, hb_ref, xcb_ref, xa_ext, u_ref, cg_ext, gp_ref, ya_ref, yb_ref,
                  hcar_ref, win_ref, wa_ref, wb_ref, wout_ref, wstage, wsem):
    step = pl.program_id(0)
    slot = step % X_SLOTS
    next_slot = (step + 1) % X_SLOTS

    def x_copies(s):
        return _time_major_copies(
            x_hbm, xbuf.at[s % X_SLOTS], xsem.at[s % X_SLOTS], s, to_vmem=True)

    def norm_chunk(src_slot, i):
        xv = xbuf[src_slot, i * TBN:(i + 1) * TBN].reshape(RBN, D_MODEL)
        hb_ref[i * RBN:(i + 1) * RBN, :] = _rms_scale(xv, gmix_ref).astype(BF16)

    @pl.when(step == 0)
    def _():
        for s in range(2):
            for cp in x_copies(s):
                cp.start()
        xa_ext[0:HALO_A, :] = jnp.zeros((HALO_A, D_MODEL), F32)
        cg_ext[0:HALO_B, :] = jnp.zeros((HALO_B, D_MODEL), F32)
        hcar_ref[...] = jnp.zeros((BATCH, D_MODEL), F32)
        _stream_cast_weights(
            [(win_hbm, win_ref, lambda c0: 1.0 if c0 < 3 * D_MODEL else 0.5),
             (wa_hbm, wa_ref, lambda c0: 0.25),
             (wb_hbm, wb_ref, lambda c0: 0.5),
             (wout_hbm, wout_ref, lambda c0: 0.5)],
            wstage, wsem)
        for cp in x_copies(0):
            cp.wait()
        for i in range(TT // TBN):
            norm_chunk(0, i)

    @pl.when(step + 2 < N_STEPS)
    def _():
        for cp in x_copies(step + 2):
            cp.start()

    @pl.when(step + 1 < N_STEPS)
    def _():
        for cp in x_copies(step + 1):
            cp.wait()

    lam = lam_ref[...]
    softplus_neg_lam = jnp.maximum(-lam, 0.0) + jnp.log(1.0 + jnp.exp(-jnp.abs(lam)))
    half_log_a_scale = (-0.5 * LRU_C) * softplus_neg_lam

    def in_proj_block(c):
        cols = _colblock(c)
        hb = hb_ref[...]
        xa_ext[HALO_A:HALO_A + M, cols] = jnp.dot(hb, win_ref[:, cols], preferred_element_type=F32)
        for g in range(5):
            wcols = slice((g + 1) * D_MODEL + c * COLB, (g + 1) * D_MODEL + (c + 1) * COLB)
            u_ref[g, :, cols] = jnp.dot(hb, win_ref[:, wcols], preferred_element_type=F32)

    def conv4_block(c):
        cols = _colblock(c)
        for s in range(M // SUB):
            r = s * SUB
            acc = b4_ref[:, cols] + w4_ref[0:1, cols] * xa_ext[r:r + SUB, cols]
            for k in range(1, LRU_CONV):
                acc = acc + w4_ref[k:k + 1, cols] * xa_ext[r + k * BATCH:r + k * BATCH + SUB, cols]
            xa_ext[r:r + SUB, cols] = acc
            xcb_ref[r:r + SUB, cols] = acc.astype(BF16)

    def gate_block(c):
        cols = _colblock(c)
        gp = jnp.dot(xcb_ref[:, cols], wg_ref[c], preferred_element_type=F32)
        gp_ref[0, :, cols] = gp[:, 0:COLB]
        gp_ref[1, :, cols] = gp[:, COLB:2 * COLB]

    def lru_block(c):
        cols = _colblock(c)
        h = hcar_ref[:, cols]
        hscale = half_log_a_scale[:, cols]
        for s in range(M // SUB):
            rows = slice(s * SUB, (s + 1) * SUB)
            xc = xa_ext[rows, cols]
            tr = jnp.tanh(gp_ref[0, rows, cols] + ba_ref[:, cols])
            ti = jnp.tanh(gp_ref[1, rows, cols] + bx_ref[:, cols])
            log_a = hscale * tr + hscale
            a = jnp.exp(log_a)
            one_minus_a2 = jnp.maximum(jnp.tanh(log_a) * (-1.0 - a * a), 0.0)
            mult = jnp.where(one_minus_a2 > 0.0, one_minus_a2 * lax.rsqrt(one_minus_a2), 0.0)
            u2 = mult * (ti * xc + xc)
            hs = []
            for q in range(SUB // BATCH):
                h = a[q * BATCH:(q + 1) * BATCH, :] * h + u2[q * BATCH:(q + 1) * BATCH, :]
                hs.append(h)
            hh = jnp.concatenate(hs, axis=0)
            ya_ref[rows, cols] = (hh * _twice_gelu(u_ref[U_GA, rows, cols])).astype(BF16)
            ca = u_ref[U_CA, rows, cols]
            cg_ext[HALO_B + s * SUB:HALO_B + (s + 1) * SUB, cols] = (
                ca * jnp.tanh(u_ref[U_CB, rows, cols]) + ca)
        hcar_ref[:, cols] = h

    def conv31_block(c):
        for lc in range(COLB // LANES):
            lanes = slice(c * COLB + lc * LANES, c * COLB + (lc + 1) * LANES)
            taps = [w31_ref[k * BATCH:(k + 1) * BATCH, lanes] for k in range(CONF_KERNEL)]
            bias = b31_ref[:, lanes]
            for tb in range(TT // TB31):
                t0 = tb * TB31
                xs = [cg_ext[(t0 + i) * BATCH:(t0 + i + 1) * BATCH, lanes]
                      for i in range(TB31 + CONF_KERNEL - 1)]
                for t in range(TB31):
                    acc = bias + taps[0] * xs[t]
                    for k in range(1, CONF_KERNEL):
                        acc = acc + taps[k] * xs[t + k]
                    u_ref[U_C31, (t0 + t) * BATCH:(t0 + t + 1) * BATCH, lanes] = acc

    for c in range(NB_D):
        in_proj_block(c)
        conv4_block(c)
        gate_block(c)
        lru_block(c)
        conv31_block(c)

    xa_ext[0:HALO_A, :] = xa_ext[M:M + HALO_A, :]
    cg_ext[0:HALO_B, :] = cg_ext[M:M + HALO_B, :]

    u_ref[U_YA] = jnp.dot(ya_ref[...], wa_ref[...], preferred_element_type=F32)

    for s in range(M // SUB):
        rows = slice(s * SUB, (s + 1) * SUB)
        acc = u_ref[U_C31, rows, :]
        mu = jnp.mean(acc, axis=-1, keepdims=True)
        cen = acc - mu
        var = jnp.mean(cen * cen, axis=-1, keepdims=True)
        y = cen * lax.rsqrt(var + EPS) * lng_ref[...] + lnb_ref[...]
        yb_ref[rows, :] = (y * jnp.tanh(0.5 * y) + y).astype(BF16)

    u_ref[U_YB] = jnp.dot(yb_ref[...], wb_ref[...], preferred_element_type=F32)

    for s in range(M // SUB):
        rows = slice(s * SUB, (s + 1) * SUB)
        ya = u_ref[U_YA, rows, :]
        yb = u_ref[U_YB, rows, :]
        ta = jnp.tanh(u_ref[U_SA, rows, :] + bgate_ref[:, 0:D_MODEL])
        tb = jnp.tanh(u_ref[U_SB, rows, :] + bgate_ref[:, D_MODEL:2 * D_MODEL])
        ya_ref[rows, :] = ((ta * ya + ya) + (tb * yb + yb)).astype(BF16)

    o_ref[...] = (xbuf[slot].reshape(M, D_MODEL)
                  + jnp.dot(ya_ref[...], wout_ref[...], preferred_element_type=F32))

    for i in range(TT // TBN):
        norm_chunk(next_slot, i)


def _ffn_kernel(x_ref, xnext_ref, gffn_ref, wup_hbm, w3_ref, b3_ref, wdown_hbm, gfin_ref,
                o_hbm,
                obuf, osem, acc_ref, hb_ref, hbn_ref, g_ext, v_ref, p_ref,
                wup_ref, wdown_ref, wstage, wsem):
    step = pl.program_id(0)

    def o_copies(s):
        return _time_major_copies(o_hbm, obuf, osem, s, to_vmem=False)

    @pl.when(step == 0)
    def _():
        g_ext[0:HALO_F, :] = jnp.zeros((HALO_F, D_FF), F32)
        _stream_cast_weights(
            [(wup_hbm, wup_ref, lambda c0: 1.0 if c0 < D_FF else 0.5),
             (wdown_hbm, wdown_ref, lambda c0: 1.0)],
            wstage, wsem)
        for i in range(M // RBN):
            rows = slice(i * RBN, (i + 1) * RBN)
            hb_ref[rows, :] = _rms_scale(x_ref[rows, :], gffn_ref).astype(BF16)

    def up_block(j):
        cols = _colblock(j)
        vcols = slice(D_FF + j * COLB, D_FF + (j + 1) * COLB)
        hb = hb_ref[...]
        g_ext[HALO_F:HALO_F + M, cols] = jnp.dot(hb, wup_ref[:, cols], preferred_element_type=F32)
        v_ref[:, cols] = jnp.dot(hb, wup_ref[:, vcols], preferred_element_type=F32)

    def act_block(j):
        cols = _colblock(j)
        for s in range(M // SUB):
            r = s * SUB
            acc = b3_ref[:, cols] + w3_ref[0:1, cols] * g_ext[r:r + SUB, cols]
            for k in range(1, FFN_CONV):
                acc = acc + w3_ref[k:k + 1, cols] * g_ext[r + k * BATCH:r + k * BATCH + SUB, cols]
            p_ref[r:r + SUB, cols] = (_twice_gelu(acc) * v_ref[r:r + SUB, cols]).astype(BF16)
        g_ext[0:HALO_F, cols] = g_ext[M:M + HALO_F, cols]

    def down_chunk(c):
        rows = slice(c * DOWN_K, (c + 1) * DOWN_K)
        part = jnp.dot(p_ref[:, rows], wdown_ref[rows, :], preferred_element_type=F32)
        acc_ref[...] = (x_ref[...] if c == 0 else acc_ref[...]) + part

    for j in range(NB_F):
        up_block(j)
        act_block(j)
        if (j + 1) % (DOWN_K // COLB) == 0:
            down_chunk(j // (DOWN_K // COLB))
        if j < M // RBN:
            rows = slice(j * RBN, (j + 1) * RBN)
            hbn_ref[rows, :] = _rms_scale(xnext_ref[rows, :], gffn_ref).astype(BF16)

    @pl.when(step > 0)
    def _():
        for cp in o_copies(step - 1):
            cp.wait()

    for i in range(TT // TBN):
        y = _rms_scale(acc_ref[i * RBN:(i + 1) * RBN, :], gfin_ref)
        obuf[i * TBN:(i + 1) * TBN] = y.reshape(TBN, BATCH, D_MODEL)

    for cp in o_copies(step):
        cp.start()

    hb_ref[...] = hbn_ref[...]

    @pl.when(step + 1 == N_STEPS)
    def _():
        for cp in o_copies(step):
            cp.wait()


def _resident(shape):
    nd = len(shape)
    return pl.BlockSpec(shape, lambda t: (0,) * nd, pipeline_mode=pl.Buffered(1))


def _row_tile(ahead=0):
    return pl.BlockSpec((M, D_MODEL), lambda t: (jnp.minimum(t + ahead, N_STEPS - 1), 0))


def _block_diag_groups(w):
    hpg = COLB // LRU_HEAD_DIM
    w4 = w.reshape(NB_D, hpg, LRU_HEAD_DIM, LRU_HEAD_DIM)
    eye = jnp.eye(hpg, dtype=w.dtype)
    return jnp.einsum("ghde,hk->ghdke", w4, eye).reshape(NB_D, COLB, COLB)


def _mixer_call(x, gmix, win, w4, b4, wg, ba, bx, lam, wa, w31, b31, lng, lnb, wb, bgate, wout):
    operands = (x, gmix, win, w4, b4, wg, ba, bx, lam, wa, w31, b31, lng, lnb, wb, bgate, wout)
    in_hbm = {0, 2, 9, 14, 16}
    in_specs = [pl.BlockSpec(memory_space=pl.ANY) if i in in_hbm else _resident(a.shape)
                for i, a in enumerate(operands)]
    return pl.pallas_call(
        _mixer_kernel,
        grid=(N_STEPS,),
        in_specs=in_specs,
        out_specs=_row_tile(),
        out_shape=jax.ShapeDtypeStruct((SEQ * BATCH, D_MODEL), F32),
        scratch_shapes=[
            pltpu.VMEM((X_SLOTS, TT, BATCH, D_MODEL), F32),
            pltpu.SemaphoreType.DMA((X_SLOTS, BATCH)),
            pltpu.VMEM((M, D_MODEL), BF16),
            pltpu.VMEM((M, D_MODEL), BF16),
            pltpu.VMEM((M + HALO_A, D_MODEL), F32),
            pltpu.VMEM((5, M, D_MODEL), F32),
            pltpu.VMEM((M + HALO_B, D_MODEL), F32),
            pltpu.VMEM((2, M, D_MODEL), F32),
            pltpu.VMEM((M, D_MODEL), BF16),
            pltpu.VMEM((M, D_MODEL), BF16),
            pltpu.VMEM((BATCH, D_MODEL), F32),
            pltpu.VMEM((D_MODEL, 6 * D_MODEL), BF16),
            pltpu.VMEM((D_MODEL, D_MODEL), BF16),
            pltpu.VMEM((D_MODEL, D_MODEL), BF16),
            pltpu.VMEM((D_MODEL, D_MODEL), BF16),
            pltpu.VMEM((2, W_CHUNK_ROWS, W_CHUNK_COLS), F32),
            pltpu.SemaphoreType.DMA((2,)),
        ],
        compiler_params=pltpu.CompilerParams(
            dimension_semantics=("arbitrary",), vmem_limit_bytes=VMEM_LIMIT_BYTES),
        name="mixer",
    )(*operands)


def _ffn_call(x1, gffn, wup, w3, b3, wdown, gfin):
    operands = (x1, x1, gffn, wup, w3, b3, wdown, gfin)
    in_hbm = {3, 6}
    in_specs = [_row_tile(), _row_tile(ahead=1)] + [
        pl.BlockSpec(memory_space=pl.ANY) if i in in_hbm else _resident(a.shape)
        for i, a in enumerate(operands) if i >= 2]
    return pl.pallas_call(
        _ffn_kernel,
        grid=(N_STEPS,),
        in_specs=in_specs,
        out_specs=pl.BlockSpec(memory_space=pl.ANY),
        out_shape=jax.ShapeDtypeStruct((BATCH, SEQ, D_MODEL), F32),
        scratch_shapes=[
            pltpu.VMEM((TT, BATCH, D_MODEL), F32),
            pltpu.SemaphoreType.DMA((BATCH,)),
            pltpu.VMEM((M, D_MODEL), F32),
            pltpu.VMEM((M, D_MODEL), BF16),
            pltpu.VMEM((M, D_MODEL), BF16),
            pltpu.VMEM((M + HALO_F, D_FF), F32),
            pltpu.VMEM((M, D_FF), F32),
            pltpu.VMEM((M, D_FF), BF16),
            pltpu.VMEM((D_MODEL, 2 * D_FF), BF16),
            pltpu.VMEM((D_FF, D_MODEL), BF16),
            pltpu.VMEM((2, W_CHUNK_ROWS, W_CHUNK_COLS), F32),
            pltpu.SemaphoreType.DMA((2,)),
        ],
        compiler_params=pltpu.CompilerParams(
            dimension_semantics=("arbitrary",), vmem_limit_bytes=VMEM_LIMIT_BYTES),
        name="ffn",
    )(*operands)


def kernel(x, g_mix, w_in, lru_conv_w, lru_conv_b, lru_wa, lru_ba, lru_wx, lru_bx, lru_lambda,
           w_lru_out, conf_dw_w, conf_dw_b, conf_ln_g, conf_ln_b, w_conf_out, b_gate, w_out,
           g_ffn, w_up, ffn_dw_w, ffn_dw_b, w_down, g_final):
    assert x.shape == (BATCH, SEQ, D_MODEL)
    assert g_mix.shape[0] == 1
    row = lambda v: v.reshape(1, -1)
    l = 0

    wg = (0.5 * jnp.concatenate(
        [_block_diag_groups(lru_wa[l]), _block_diag_groups(lru_wx[l])], axis=-1)).astype(BF16)
    w31 = jnp.repeat(0.5 * conf_dw_w[l], BATCH, axis=0)
    x1 = _mixer_call(
        x, row(g_mix[l]), w_in[l], lru_conv_w[l], row(lru_conv_b[l]), wg,
        row(0.5 * lru_ba[l]), row(0.5 * lru_bx[l]), row(lru_lambda[l]),
        w_lru_out[l], w31, row(conf_dw_b[l]), row(conf_ln_g[l]),
        row(conf_ln_b[l]), w_conf_out[l], row(0.5 * b_gate[l]), w_out[l])
    return _ffn_call(
        x1, row(g_ffn[l]), w_up[l], ffn_dw_w[l], row(ffn_dw_b[l]), w_down[l], row(g_final))
```

```python
import jax
import jax.numpy as jnp
from jax import lax
from jax.experimental import pallas as pl
from jax.experimental.pallas import tpu as pltpu

D_MODEL = 1024
BATCH = 8
SEQ = 4096
LRU_HEADS = 16
LRU_HEAD_DIM = D_MODEL // LRU_HEADS
LRU_CONV = 4
LRU_C = 8.0
CONF_KERNEL = 31
D_FF = 3 * D_MODEL
FFN_CONV = 3
EPS = 1e-6

F32 = jnp.float32
BF16 = jnp.bfloat16

LANES = 128
TT = 64
N_STEPS = SEQ // TT
M = TT * BATCH
TBN = 16
RBN = TBN * BATCH
SUB = 128
COLB = 256
NB_D = D_MODEL // COLB
NB_F = D_FF // COLB
DOWN_K = 1024
TB31 = 8
X_SLOTS = 3
W_CHUNK_ROWS = 1024
W_CHUNK_COLS = 512
HALO_A = (LRU_CONV - 1) * BATCH
HALO_B = (CONF_KERNEL - 1) * BATCH
HALO_F = (FFN_CONV - 1) * BATCH
VMEM_LIMIT_BYTES = 58 * 1024 * 1024
GELU_C0 = 0.7978845608028654
GELU_C1 = 0.044715 * GELU_C0

U_GA, U_CA, U_CB, U_SA, U_SB = range(5)
U_YA, U_YB, U_C31 = U_GA, U_CA, U_CB


def _twice_gelu(z):
    return z + z * jnp.tanh(z * (GELU_C0 + GELU_C1 * (z * z)))


def _rms_scale(xv, gain_ref):
    ms = jnp.mean(xv * xv, axis=-1, keepdims=True)
    return xv * lax.rsqrt(ms + EPS) * gain_ref[...]


def _colblock(j):
    return slice(j * COLB, (j + 1) * COLB)


def _time_major_copies(hbm_ref, vmem_ref, sem_ref, step, to_vmem):
    copies = []
    for b in range(BATCH):
        hbm = hbm_ref.at[b, pl.ds(step * TT, TT), :]
        vmem = vmem_ref.at[:, b, :]
        src, dst = (hbm, vmem) if to_vmem else (vmem, hbm)
        copies.append(pltpu.make_async_copy(src, dst, sem_ref.at[b]))
    return copies


def _stream_cast_weights(jobs, stage, sem):
    chunks = []
    for src, dst, column_scale in jobs:
        n_rows, n_cols = dst.shape
        for r0 in range(0, n_rows, W_CHUNK_ROWS):
            for c0 in range(0, n_cols, W_CHUNK_COLS):
                chunks.append((src, dst, r0, c0, column_scale(c0)))

    def copy(i):
        src, _, r0, c0, _ = chunks[i]
        return pltpu.make_async_copy(
            src.at[pl.ds(r0, W_CHUNK_ROWS), pl.ds(c0, W_CHUNK_COLS)], stage.at[i % 2], sem.at[i % 2])

    copy(0).start()
    for i, (_, dst, r0, c0, scale) in enumerate(chunks):
        if i + 1 < len(chunks):
            copy(i + 1).start()
        copy(i).wait()
        w = stage[i % 2]
        if scale != 1.0:
            w = w * scale
        dst[r0:r0 + W_CHUNK_ROWS, c0:c0 + W_CHUNK_COLS] = w.astype(BF16)


def _mixer_kernel(x_hbm, gmix_ref, win_hbm, w4_ref, b4_ref, wg_ref, ba_ref, bx_ref, lam_ref,
                  wa_hbm, w31_ref, b31_ref, lng_ref, lnb_ref, wb_hbm, bgate_ref, wout_hbm,
                  o_ref,
                  xbuf, xsem, hb_ref, xcb_ref, xa_ext, u_ref, cg_ext, gp_ref, ya_ref, yb_ref,
                  hcar_ref, win_ref, wa_ref, wb_ref, wout_ref, wstage, wsem):
    step = pl.program_id(0)
    slot = step % X_SLOTS
    next_slot = (step + 1) % X_SLOTS

    def x_copies(s):
        return _time_major_copies(
            x_hbm, xbuf.at[s % X_SLOTS], xsem.at[s % X_SLOTS], s, to_vmem=True)

    def norm_chunk(src_slot, i):
        xv = xbuf[src_slot, i * TBN:(i + 1) * TBN].reshape(RBN, D_MODEL)
        hb_ref[i * RBN:(i + 1) * RBN, :] = _rms_scale(xv, gmix_ref).astype(BF16)

    @pl.when(step == 0)
    def _():
        for s in range(2):
            for cp in x_copies(s):
                cp.start()
        xa_ext[0:HALO_A, :] = jnp.zeros((HALO_A, D_MODEL), F32)
        cg_ext[0:HALO_B, :] = jnp.zeros((HALO_B, D_MODEL), F32)
        hcar_ref[...] = jnp.zeros((BATCH, D_MODEL), F32)
        _stream_cast_weights(
            [(win_hbm, win_ref, lambda c0: 1.0 if c0 < 3 * D_MODEL else 0.5),
             (wa_hbm, wa_ref, lambda c0: 0.25),
             (wb_hbm, wb_ref, lambda c0: 0.5),
             (wout_hbm, wout_ref, lambda c0: 0.5)],
            wstage, wsem)
        for cp in x_copies(0):
            cp.wait()
        for i in range(TT // TBN):
            norm_chunk(0, i)

    @pl.when(step + 2 < N_STEPS)
    def _():
        for cp in x_copies(step + 2):
            cp.start()

    @pl.when(step + 1 < N_STEPS)
    def _():
        for cp in x_copies(step + 1):
            cp.wait()

    lam = lam_ref[...]
    softplus_neg_lam = jnp.maximum(-lam, 0.0) + jnp.log(1.0 + jnp.exp(-jnp.abs(lam)))
    half_log_a_scale = (-0.5 * LRU_C) * softplus_neg_lam

    def in_proj_block(c):
        cols = _colblock(c)
        hb = hb_ref[...]
        xa_ext[HALO_A:HALO_A + M, cols] = jnp.dot(hb, win_ref[:, cols], preferred_element_type=F32)
        for g in range(5):
            wcols = slice((g + 1) * D_MODEL + c * COLB, (g + 1) * D_MODEL + (c + 1) * COLB)
            u_ref[g, :, cols] = jnp.dot(hb, win_ref[:, wcols], preferred_element_type=F32)

    def conv4_block(c):
        cols = _colblock(c)
        for s in range(M // SUB):
            r = s * SUB
            acc = b4_ref[:, cols] + w4_ref[0:1, cols] * xa_ext[r:r + SUB, cols]
            for k in range(1, LRU_CONV):
                acc = acc + w4_ref[k:k + 1, cols] * xa_ext[r + k * BATCH:r + k * BATCH + SUB, cols]
            xa_ext[r:r + SUB, cols] = acc
            xcb_ref[r:r + SUB, cols] = acc.astype(BF16)

    def gate_block(c):
        cols = _colblock(c)
        gp = jnp.dot(xcb_ref[:, cols], wg_ref[c], preferred_element_type=F32)
        gp_ref[0, :, cols] = gp[:, 0:COLB]
        gp_ref[1, :, cols] = gp[:, COLB:2 * COLB]

    def lru_block(c):
        cols = _colblock(c)
        h = hcar_ref[:, cols]
        hscale = half_log_a_scale[:, cols]
        for s in range(M // SUB):
            rows = slice(s * SUB, (s + 1) * SUB)
            xc = xa_ext[rows, cols]
            tr = jnp.tanh(gp_ref[0, rows, cols] + ba_ref[:, cols])
            ti = jnp.tanh(gp_ref[1, rows, cols] + bx_ref[:, cols])
            log_a = hscale * tr + hscale
            a = jnp.exp(log_a)
            one_minus_a2 = jnp.maximum(jnp.tanh(log_a) * (-1.0 - a * a), 0.0)
            mult = jnp.where(one_minus_a2 > 0.0, one_minus_a2 * lax.rsqrt(one_minus_a2), 0.0)
            u2 = mult * (ti * xc + xc)
            hs = []
            for q in range(SUB // BATCH):
                h = a[q * BATCH:(q + 1) * BATCH, :] * h + u2[q * BATCH:(q + 1) * BATCH, :]
                hs.append(h)
            hh = jnp.concatenate(hs, axis=0)
            ya_ref[rows, cols] = (hh * _twice_gelu(u_ref[U_GA, rows, cols])).astype(BF16)
            ca = u_ref[U_CA, rows, cols]
            cg_ext[HALO_B + s * SUB:HALO_B + (s + 1) * SUB, cols] = (
                ca * jnp.tanh(u_ref[U_CB, rows, cols]) + ca)
        hcar_ref[:, cols] = h

    def conv31_block(c):
        for lc in range(COLB // LANES):
            lanes = slice(c * COLB + lc * LANES, c * COLB + (lc + 1) * LANES)
            taps = [w31_ref[k * BATCH:(k + 1) * BATCH, lanes] for k in range(CONF_KERNEL)]
            bias = b31_ref[:, lanes]
            for tb in range(TT // TB31):
                t0 = tb * TB31
                xs = [cg_ext[(t0 + i) * BATCH:(t0 + i + 1) * BATCH, lanes]
                      for i in range(TB31 + CONF_KERNEL - 1)]
                for t in range(TB31):
                    acc = bias + taps[0] * xs[t]
                    for k in range(1, CONF_KERNEL):
                        acc = acc + taps[k] * xs[t + k]
                    u_ref[U_C31, (t0 + t) * BATCH:(t0 + t + 1) * BATCH, lanes] = acc

    for c in range(NB_D):
        in_proj_block(c)
        conv4_block(c)
        gate_block(c)
        lru_block(c)
        conv31_block(c)

    xa_ext[0:HALO_A, :] = xa_ext[M:M + HALO_A, :]
    cg_ext[0:HALO_B, :] = cg_ext[M:M + HALO_B, :]

    u_ref[U_YA] = jnp.dot(ya_ref[...], wa_ref[...], preferred_element_type=F32)

    for s in range(M // SUB):
        rows = slice(s * SUB, (s + 1) * SUB)
        acc = u_ref[U_C31, rows, :]
        mu = jnp.mean(acc, axis=-1, keepdims=True)
        cen = acc - mu
        var = jnp.mean(cen * cen, axis=-1, keepdims=True)
        y = cen * lax.rsqrt(var + EPS) * lng_ref[...] + lnb_ref[...]
        yb_ref[rows, :] = (y * jnp.tanh(0.5 * y) + y).astype(BF16)

    u_ref[U_YB] = jnp.dot(yb_ref[...], wb_ref[...], preferred_element_type=F32)

    for s in range(M // SUB):
        rows = slice(s * SUB, (s + 1) * SUB)
        ya = u_ref[U_YA, rows, :]
        yb = u_ref[U_YB, rows, :]
        ta = jnp.tanh(u_ref[U_SA, rows, :] + bgate_ref[:, 0:D_MODEL])
        tb = jnp.tanh(u_ref[U_SB, rows, :] + bgate_ref[:, D_MODEL:2 * D_MODEL])
        ya_ref[rows, :] = ((ta * ya + ya) + (tb * yb + yb)).astype(BF16)

    o_ref[...] = (xbuf[slot].reshape(M, D_MODEL)
                  + jnp.dot(ya_ref[...], wout_ref[...], preferred_element_type=F32))

    for i in range(TT // TBN):
        norm_chunk(next_slot, i)


def _ffn_kernel(x_ref, xnext_ref, gffn_ref, wup_hbm, w3_ref, b3_ref, wdown_hbm, gfin_ref,
                o_hbm,
                obuf, osem, acc_ref, hb_ref, g_ext, v_ref, p_ref,
                wup_ref, wdown_ref, wstage, wsem):
    step = pl.program_id(0)

    def o_copies(s):
        return _time_major_copies(o_hbm, obuf, osem, s, to_vmem=False)

    @pl.when(step == 0)
    def _():
        g_ext[0:HALO_F, :] = jnp.zeros((HALO_F, D_FF), F32)
        _stream_cast_weights(
            [(wup_hbm, wup_ref, lambda c0: 1.0 if c0 < D_FF else 0.5),
             (wdown_hbm, wdown_ref, lambda c0: 1.0)],
            wstage, wsem)
        for i in range(M // RBN):
            rows = slice(i * RBN, (i + 1) * RBN)
            hb_ref[rows, :] = _rms_scale(x_ref[rows, :], gffn_ref).astype(BF16)

    def up_block(j):
        cols = _colblock(j)
        vcols = slice(D_FF + j * COLB, D_FF + (j + 1) * COLB)
        hb = hb_ref[...]
        g_ext[HALO_F:HALO_F + M, cols] = jnp.dot(hb, wup_ref[:, cols], preferred_element_type=F32)
        v_ref[:, cols] = jnp.dot(hb, wup_ref[:, vcols], preferred_element_type=F32)

    def act_block(j):
        cols = _colblock(j)
        for s in range(M // SUB):
            r = s * SUB
            acc = b3_ref[:, cols] + w3_ref[0:1, cols] * g_ext[r:r + SUB, cols]
            for k in range(1, FFN_CONV):
                acc = acc + w3_ref[k:k + 1, cols] * g_ext[r + k * BATCH:r + k * BATCH + SUB, cols]
            p_ref[r:r + SUB, cols] = (_twice_gelu(acc) * v_ref[r:r + SUB, cols]).astype(BF16)
        g_ext[0:HALO_F, cols] = g_ext[M:M + HALO_F, cols]

    def down_chunk(c):
        rows = slice(c * DOWN_K, (c + 1) * DOWN_K)
        part = jnp.dot(p_ref[:, rows], wdown_ref[rows, :], preferred_element_type=F32)
        acc_ref[...] = (x_ref[...] if c == 0 else acc_ref[...]) + part

    for j in range(NB_F):
        up_block(j)
        act_block(j)
        if (j + 1) % (DOWN_K // COLB) == 0:
            down_chunk(j // (DOWN_K // COLB))

    for i in range(M // RBN):
        rows = slice(i * RBN, (i + 1) * RBN)
        hb_ref[rows, :] = _rms_scale(xnext_ref[rows, :], gffn_ref).astype(BF16)

    @pl.when(step > 0)
    def _():
        for cp in o_copies(step - 1):
            cp.wait()

    for i in range(TT // TBN):
        y = _rms_scale(acc_ref[i * RBN:(i + 1) * RBN, :], gfin_ref)
        obuf[i * TBN:(i + 1) * TBN] = y.reshape(TBN, BATCH, D_MODEL)

    for cp in o_copies(step):
        cp.start()

    @pl.when(step + 1 == N_STEPS)
    def _():
        for cp in o_copies(step):
            cp.wait()


def _resident(shape):
    nd = len(shape)
    return pl.BlockSpec(shape, lambda t: (0,) * nd, pipeline_mode=pl.Buffered(1))


def _row_tile(ahead=0):
    return pl.BlockSpec((M, D_MODEL), lambda t: (jnp.minimum(t + ahead, N_STEPS - 1), 0))


def _block_diag_groups(w):
    hpg = COLB // LRU_HEAD_DIM
    w4 = w.reshape(NB_D, hpg, LRU_HEAD_DIM, LRU_HEAD_DIM)
    eye = jnp.eye(hpg, dtype=w.dtype)
    return jnp.einsum("ghde,hk->ghdke", w4, eye).reshape(NB_D, COLB, COLB)


def _mixer_call(x, gmix, win, w4, b4, wg, ba, bx, lam, wa, w31, b31, lng, lnb, wb, bgate, wout):
    operands = (x, gmix, win, w4, b4, wg, ba, bx, lam, wa, w31, b31, lng, lnb, wb, bgate, wout)
    in_hbm = {0, 2, 9, 14, 16}
    in_specs = [pl.BlockSpec(memory_space=pl.ANY) if i in in_hbm else _resident(a.shape)
                for i, a in enumerate(operands)]
    return pl.pallas_call(
        _mixer_kernel,
        grid=(N_STEPS,),
        in_specs=in_specs,
        out_specs=_row_tile(),
        out_shape=jax.ShapeDtypeStruct((SEQ * BATCH, D_MODEL), F32),
        scratch_shapes=[
            pltpu.VMEM((X_SLOTS, TT, BATCH, D_MODEL), F32),
            pltpu.SemaphoreType.DMA((X_SLOTS, BATCH)),
            pltpu.VMEM((M, D_MODEL), BF16),
            pltpu.VMEM((M, D_MODEL), BF16),
            pltpu.VMEM((M + HALO_A, D_MODEL), F32),
            pltpu.VMEM((5, M, D_MODEL), F32),
            pltpu.VMEM((M + HALO_B, D_MODEL), F32),
            pltpu.VMEM((2, M, D_MODEL), F32),
            pltpu.VMEM((M, D_MODEL), BF16),
            pltpu.VMEM((M, D_MODEL), BF16),
            pltpu.VMEM((BATCH, D_MODEL), F32),
            pltpu.VMEM((D_MODEL, 6 * D_MODEL), BF16),
            pltpu.VMEM((D_MODEL, D_MODEL), BF16),
            pltpu.VMEM((D_MODEL, D_MODEL), BF16),
            pltpu.VMEM((D_MODEL, D_MODEL), BF16),
            pltpu.VMEM((2, W_CHUNK_ROWS, W_CHUNK_COLS), F32),
            pltpu.SemaphoreType.DMA((2,)),
        ],
        compiler_params=pltpu.CompilerParams(
            dimension_semantics=("arbitrary",), vmem_limit_bytes=VMEM_LIMIT_BYTES),
        name="mixer",
    )(*operands)


def _ffn_call(x1, gffn, wup, w3, b3, wdown, gfin):
    operands = (x1, x1, gffn, wup, w3, b3, wdown, gfin)
    in_hbm = {3, 6}
    in_specs = [_row_tile(), _row_tile(ahead=1)] + [
        pl.BlockSpec(memory_space=pl.ANY) if i in in_hbm else _resident(a.shape)
        for i, a in enumerate(operands) if i >= 2]
    return pl.pallas_call(
        _ffn_kernel,
        grid=(N_STEPS,),
        in_specs=in_specs,
        out_specs=pl.BlockSpec(memory_space=pl.ANY),
        out_shape=jax.ShapeDtypeStruct((BATCH, SEQ, D_MODEL), F32),
        scratch_shapes=[
            pltpu.VMEM((TT, BATCH, D_MODEL), F32),
            pltpu.SemaphoreType.DMA((BATCH,)),
            pltpu.VMEM((M, D_MODEL), F32),
            pltpu.VMEM((M, D_MODEL), BF16),
            pltpu.VMEM((M + HALO_F, D_FF), F32),
            pltpu.VMEM((M, D_FF), F32),
            pltpu.VMEM((M, D_FF), BF16),
            pltpu.VMEM((D_MODEL, 2 * D_FF), BF16),
            pltpu.VMEM((D_FF, D_MODEL), BF16),
            pltpu.VMEM((2, W_CHUNK_ROWS, W_CHUNK_COLS), F32),
            pltpu.SemaphoreType.DMA((2,)),
        ],
        compiler_params=pltpu.CompilerParams(
            dimension_semantics=("arbitrary",), vmem_limit_bytes=VMEM_LIMIT_BYTES),
        name="ffn",
    )(*operands)


def kernel(x, g_mix, w_in, lru_conv_w, lru_conv_b, lru_wa, lru_ba, lru_wx, lru_bx, lru_lambda,
           w_lru_out, conf_dw_w, conf_dw_b, conf_ln_g, conf_ln_b, w_conf_out, b_gate, w_out,
           g_ffn, w_up, ffn_dw_w, ffn_dw_b, w_down, g_final):
    assert x.shape == (BATCH, SEQ, D_MODEL)
    assert g_mix.shape[0] == 1
    row = lambda v: v.reshape(1, -1)
    l = 0

    wg = (0.5 * jnp.concatenate(
        [_block_diag_groups(lru_wa[l]), _block_diag_groups(lru_wx[l])], axis=-1)).astype(BF16)
    w31 = jnp.repeat(0.5 * conf_dw_w[l], BATCH, axis=0)
    x1 = _mixer_call(
        x, row(g_mix[l]), w_in[l], lru_conv_w[l], row(lru_conv_b[l]), wg,
        row(0.5 * lru_ba[l]), row(0.5 * lru_bx[l]), row(lru_lambda[l]),
        w_lru_out[l], w31, row(conf_dw_b[l]), row(conf_ln_g[l]),
        row(conf_ln_b[l]), w_conf_out[l], row(0.5 * b_gate[l]), w_out[l])
    return _ffn_call(
        x1, row(g_ffn[l]), w_up[l], ffn_dw_w[l], row(ffn_dw_b[l]), w_down[l], row(g_final))
```

```python
import jax
import jax.numpy as jnp
from jax import lax
from jax.experimental import pallas as pl
from jax.experimental.pallas import tpu as pltpu

D_MODEL = 1024
BATCH = 8
SEQ = 4096
LRU_HEADS = 16
LRU_HEAD_DIM = D_MODEL // LRU_HEADS
LRU_CONV = 4
LRU_C = 8.0
CONF_KERNEL = 31
D_FF = 3 * D_MODEL
FFN_CONV = 3
EPS = 1e-6

F32 = jnp.float32
BF16 = jnp.bfloat16

LANES = 128
TT = 64
N_STEPS = SEQ // TT
M = TT * BATCH
TBN = 16
RBN = TBN * BATCH
SUB = 128
COLB = 256
NB_D = D_MODEL // COLB
NB_F = D_FF // COLB
DOWN_K = 1024
TB31 = 8
X_SLOTS = 3
W_CHUNK_ROWS = 1024
W_CHUNK_COLS = 512
HALO_A = (LRU_CONV - 1) * BATCH
HALO_B = (CONF_KERNEL - 1) * BATCH
HALO_F = (FFN_CONV - 1) * BATCH
VMEM_LIMIT_BYTES = 58 * 1024 * 1024
GELU_C0 = 0.7978845608028654
GELU_C1 = 0.044715 * GELU_C0

U_GA, U_CA, U_CB, U_SA, U_SB = range(5)
U_YA, U_YB, U_C31 = U_GA, U_CA, U_CB


def _twice_gelu(z):
    return z + z * jnp.tanh(z * (GELU_C0 + GELU_C1 * (z * z)))


def _rms_scale(xv, gain_ref):
    ms = jnp.mean(xv * xv, axis=-1, keepdims=True)
    return xv * lax.rsqrt(ms + EPS) * gain_ref[...]


def _colblock(j):
    return slice(j * COLB, (j + 1) * COLB)


def _time_major_copies(hbm_ref, vmem_ref, sem_ref, step, to_vmem):
    copies = []
    for b in range(BATCH):
        hbm = hbm_ref.at[b, pl.ds(step * TT, TT), :]
        vmem = vmem_ref.at[:, b, :]
        src, dst = (hbm, vmem) if to_vmem else (vmem, hbm)
        copies.append(pltpu.make_async_copy(src, dst, sem_ref.at[b]))
    return copies


def _stream_cast_weights(jobs, stage, sem):
    chunks = []
    for src, dst, column_scale in jobs:
        n_rows, n_cols = dst.shape
        for r0 in range(0, n_rows, W_CHUNK_ROWS):
            for c0 in range(0, n_cols, W_CHUNK_COLS):
                chunks.append((src, dst, r0, c0, column_scale(c0)))

    def copy(i):
        src, _, r0, c0, _ = chunks[i]
        return pltpu.make_async_copy(
            src.at[pl.ds(r0, W_CHUNK_ROWS), pl.ds(c0, W_CHUNK_COLS)], stage.at[i % 2], sem.at[i % 2])

    copy(0).start()
    for i, (_, dst, r0, c0, scale) in enumerate(chunks):
        if i + 1 < len(chunks):
            copy(i + 1).start()
        copy(i).wait()
        w = stage[i % 2]
        if scale != 1.0:
            w = w * scale
        dst[r0:r0 + W_CHUNK_ROWS, c0:c0 + W_CHUNK_COLS] = w.astype(BF16)


def _mixer_kernel(x_hbm, gmix_ref, win_hbm, w4_ref, b4_ref, wg_ref, ba_ref, bx_ref, lam_ref,
                  wa_hbm, w31_ref, b31_ref, lng_ref, lnb_ref, wb_hbm, bgate_ref, wout_hbm,
                  o_ref,
                  xbuf, xsem, hb_ref, xcb_ref, xa_ext, u_ref, cg_ext, gp_ref, ya_ref, yb_ref,
                  hcar_ref, win_ref, wa_ref, wb_ref, wout_ref, wstage, wsem):
    step = pl.program_id(0)
    slot = step % X_SLOTS
    next_slot = (step + 1) % X_SLOTS

    def x_copies(s):
        return _time_major_copies(
            x_hbm, xbuf.at[s % X_SLOTS], xsem.at[s % X_SLOTS], s, to_vmem=True)

    def norm_chunk(src_slot, i):
        xv = xbuf[src_slot, i * TBN:(i + 1) * TBN].reshape(RBN, D_MODEL)
        hb_ref[i * RBN:(i + 1) * RBN, :] = _rms_scale(xv, gmix_ref).astype(BF16)

    @pl.when(step == 0)
    def _():
        for s in range(2):
            for b, cp in enumerate(x_copies(s)):
                cp.start(priority=b % 2)
        xa_ext[0:HALO_A, :] = jnp.zeros((HALO_A, D_MODEL), F32)
        cg_ext[0:HALO_B, :] = jnp.zeros((HALO_B, D_MODEL), F32)
        hcar_ref[...] = jnp.zeros((BATCH, D_MODEL), F32)
        _stream_cast_weights(
            [(win_hbm, win_ref, lambda c0: 1.0 if c0 < 3 * D_MODEL else 0.5),
             (wa_hbm, wa_ref, lambda c0: 0.25),
             (wb_hbm, wb_ref, lambda c0: 0.5),
             (wout_hbm, wout_ref, lambda c0: 0.5)],
            wstage, wsem)
        for cp in x_copies(0):
            cp.wait()
        for i in range(TT // TBN):
            norm_chunk(0, i)

    @pl.when(step + 2 < N_STEPS)
    def _():
        for b, cp in enumerate(x_copies(step + 2)):
            cp.start(priority=b % 2)

    @pl.when(step + 1 < N_STEPS)
    def _():
        for cp in x_copies(step + 1):
            cp.wait()

    lam = lam_ref[...]
    softplus_neg_lam = jnp.maximum(-lam, 0.0) + jnp.log(1.0 + jnp.exp(-jnp.abs(lam)))
    half_log_a_scale = (-0.5 * LRU_C) * softplus_neg_lam

    def in_proj_block(c):
        cols = _colblock(c)
        hb = hb_ref[...]
        xa_ext[HALO_A:HALO_A + M, cols] = jnp.dot(hb, win_ref[:, cols], preferred_element_type=F32)
        for g in range(5):
            wcols = slice((g + 1) * D_MODEL + c * COLB, (g + 1) * D_MODEL + (c + 1) * COLB)
            u_ref[g, :, cols] = jnp.dot(hb, win_ref[:, wcols], preferred_element_type=F32)

    def conv4_block(c):
        cols = _colblock(c)
        for s in range(M // SUB):
            r = s * SUB
            acc = b4_ref[:, cols] + w4_ref[0:1, cols] * xa_ext[r:r + SUB, cols]
            for k in range(1, LRU_CONV):
                acc = acc + w4_ref[k:k + 1, cols] * xa_ext[r + k * BATCH:r + k * BATCH + SUB, cols]
            xa_ext[r:r + SUB, cols] = acc
            xcb_ref[r:r + SUB, cols] = acc.astype(BF16)

    def gate_block(c):
        cols = _colblock(c)
        gp = jnp.dot(xcb_ref[:, cols], wg_ref[c], preferred_element_type=F32)
        gp_ref[0, :, cols] = gp[:, 0:COLB]
        gp_ref[1, :, cols] = gp[:, COLB:2 * COLB]

    def lru_block(c):
        cols = _colblock(c)
        h = hcar_ref[:, cols]
        hscale = half_log_a_scale[:, cols]
        for s in range(M // SUB):
            rows = slice(s * SUB, (s + 1) * SUB)
            xc = xa_ext[rows, cols]
            tr = jnp.tanh(gp_ref[0, rows, cols] + ba_ref[:, cols])
            ti = jnp.tanh(gp_ref[1, rows, cols] + bx_ref[:, cols])
            log_a = hscale * tr + hscale
            a = jnp.exp(log_a)
            one_minus_a2 = jnp.maximum(jnp.tanh(log_a) * (-1.0 - a * a), 0.0)
            mult = jnp.where(one_minus_a2 > 0.0, one_minus_a2 * lax.rsqrt(one_minus_a2), 0.0)
            u2 = mult * (ti * xc + xc)
            hs = []
            for q in range(SUB // BATCH):
                h = a[q * BATCH:(q + 1) * BATCH, :] * h + u2[q * BATCH:(q + 1) * BATCH, :]
                hs.append(h)
            hh = jnp.concatenate(hs, axis=0)
            ya_ref[rows, cols] = (hh * _twice_gelu(u_ref[U_GA, rows, cols])).astype(BF16)
            ca = u_ref[U_CA, rows, cols]
            cg_ext[HALO_B + s * SUB:HALO_B + (s + 1) * SUB, cols] = (
                ca * jnp.tanh(u_ref[U_CB, rows, cols]) + ca)
        hcar_ref[:, cols] = h

    def conv31_block(c):
        for lc in range(COLB // LANES):
            lanes = slice(c * COLB + lc * LANES, c * COLB + (lc + 1) * LANES)
            taps = [w31_ref[k * BATCH:(k + 1) * BATCH, lanes] for k in range(CONF_KERNEL)]
            bias = b31_ref[:, lanes]
            for tb in range(TT // TB31):
                t0 = tb * TB31
                xs = [cg_ext[(t0 + i) * BATCH:(t0 + i + 1) * BATCH, lanes]
                      for i in range(TB31 + CONF_KERNEL - 1)]
                for t in range(TB31):
                    acc = bias + taps[0] * xs[t]
                    for k in range(1, CONF_KERNEL):
                        acc = acc + taps[k] * xs[t + k]
                    u_ref[U_C31, (t0 + t) * BATCH:(t0 + t + 1) * BATCH, lanes] = acc

    for c in range(NB_D):
        in_proj_block(c)
        conv4_block(c)
        gate_block(c)
        lru_block(c)
        conv31_block(c)

    xa_ext[0:HALO_A, :] = xa_ext[M:M + HALO_A, :]
    cg_ext[0:HALO_B, :] = cg_ext[M:M + HALO_B, :]

    u_ref[U_YA] = jnp.dot(ya_ref[...], wa_ref[...], preferred_element_type=F32)

    for s in range(M // SUB):
        rows = slice(s * SUB, (s + 1) * SUB)
        acc = u_ref[U_C31, rows, :]
        mu = jnp.mean(acc, axis=-1, keepdims=True)
        cen = acc - mu
        var = jnp.mean(cen * cen, axis=-1, keepdims=True)
        y = cen * lax.rsqrt(var + EPS) * lng_ref[...] + lnb_ref[...]
        yb_ref[rows, :] = (y * jnp.tanh(0.5 * y) + y).astype(BF16)

    u_ref[U_YB] = jnp.dot(yb_ref[...], wb_ref[...], preferred_element_type=F32)

    for s in range(M // SUB):
        rows = slice(s * SUB, (s + 1) * SUB)
        ya = u_ref[U_YA, rows, :]
        yb = u_ref[U_YB, rows, :]
        ta = jnp.tanh(u_ref[U_SA, rows, :] + bgate_ref[:, 0:D_MODEL])
        tb = jnp.tanh(u_ref[U_SB, rows, :] + bgate_ref[:, D_MODEL:2 * D_MODEL])
        ya_ref[rows, :] = ((ta * ya + ya) + (tb * yb + yb)).astype(BF16)

    o_ref[...] = (xbuf[slot].reshape(M, D_MODEL)
                  + jnp.dot(ya_ref[...], wout_ref[...], preferred_element_type=F32))

    for i in range(TT // TBN):
        norm_chunk(next_slot, i)


def _ffn_kernel(x_ref, xnext_ref, gffn_ref, wup_hbm, w3_ref, b3_ref, wdown_hbm, gfin_ref,
                o_hbm,
                obuf, osem, acc_ref, hb_ref, g_ext, v_ref, p_ref,
                wup_ref, wdown_ref, wstage, wsem):
    step = pl.program_id(0)

    def o_copies(s):
        return _time_major_copies(o_hbm, obuf, osem, s, to_vmem=False)

    @pl.when(step == 0)
    def _():
        g_ext[0:HALO_F, :] = jnp.zeros((HALO_F, D_FF), F32)
        _stream_cast_weights(
            [(wup_hbm, wup_ref, lambda c0: 1.0 if c0 < D_FF else 0.5),
             (wdown_hbm, wdown_ref, lambda c0: 1.0)],
            wstage, wsem)
        for i in range(M // RBN):
            rows = slice(i * RBN, (i + 1) * RBN)
            hb_ref[rows, :] = _rms_scale(x_ref[rows, :], gffn_ref).astype(BF16)

    def up_block(j):
        cols = _colblock(j)
        vcols = slice(D_FF + j * COLB, D_FF + (j + 1) * COLB)
        hb = hb_ref[...]
        g_ext[HALO_F:HALO_F + M, cols] = jnp.dot(hb, wup_ref[:, cols], preferred_element_type=F32)
        v_ref[:, cols] = jnp.dot(hb, wup_ref[:, vcols], preferred_element_type=F32)

    def act_block(j):
        cols = _colblock(j)
        for s in range(M // SUB):
            r = s * SUB
            acc = b3_ref[:, cols] + w3_ref[0:1, cols] * g_ext[r:r + SUB, cols]
            for k in range(1, FFN_CONV):
                acc = acc + w3_ref[k:k + 1, cols] * g_ext[r + k * BATCH:r + k * BATCH + SUB, cols]
            p_ref[r:r + SUB, cols] = (_twice_gelu(acc) * v_ref[r:r + SUB, cols]).astype(BF16)
        g_ext[0:HALO_F, cols] = g_ext[M:M + HALO_F, cols]

    def down_chunk(c):
        rows = slice(c * DOWN_K, (c + 1) * DOWN_K)
        part = jnp.dot(p_ref[:, rows], wdown_ref[rows, :], preferred_element_type=F32)
        acc_ref[...] = (x_ref[...] if c == 0 else acc_ref[...]) + part

    for j in range(NB_F):
        up_block(j)
        act_block(j)
        if (j + 1) % (DOWN_K // COLB) == 0:
            down_chunk(j // (DOWN_K // COLB))

    for i in range(M // RBN):
        rows = slice(i * RBN, (i + 1) * RBN)
        hb_ref[rows, :] = _rms_scale(xnext_ref[rows, :], gffn_ref).astype(BF16)

    @pl.when(step > 0)
    def _():
        for cp in o_copies(step - 1):
            cp.wait()

    for i in range(TT // TBN):
        y = _rms_scale(acc_ref[i * RBN:(i + 1) * RBN, :], gfin_ref)
        obuf[i * TBN:(i + 1) * TBN] = y.reshape(TBN, BATCH, D_MODEL)

    for b, cp in enumerate(o_copies(step)):
        cp.start(priority=b % 2)

    @pl.when(step + 1 == N_STEPS)
    def _():
        for cp in o_copies(step):
            cp.wait()


def _resident(shape):
    nd = len(shape)
    return pl.BlockSpec(shape, lambda t: (0,) * nd, pipeline_mode=pl.Buffered(1))


def _row_tile(ahead=0):
    return pl.BlockSpec((M, D_MODEL), lambda t: (jnp.minimum(t + ahead, N_STEPS - 1), 0))


def _block_diag_groups(w):
    hpg = COLB // LRU_HEAD_DIM
    w4 = w.reshape(NB_D, hpg, LRU_HEAD_DIM, LRU_HEAD_DIM)
    eye = jnp.eye(hpg, dtype=w.dtype)
    return jnp.einsum("ghde,hk->ghdke", w4, eye).reshape(NB_D, COLB, COLB)


def _mixer_call(x, gmix, win, w4, b4, wg, ba, bx, lam, wa, w31, b31, lng, lnb, wb, bgate, wout):
    operands = (x, gmix, win, w4, b4, wg, ba, bx, lam, wa, w31, b31, lng, lnb, wb, bgate, wout)
    in_hbm = {0, 2, 9, 14, 16}
    in_specs = [pl.BlockSpec(memory_space=pl.ANY) if i in in_hbm else _resident(a.shape)
                for i, a in enumerate(operands)]
    return pl.pallas_call(
        _mixer_kernel,
        grid=(N_STEPS,),
        in_specs=in_specs,
        out_specs=_row_tile(),
        out_shape=jax.ShapeDtypeStruct((SEQ * BATCH, D_MODEL), F32),
        scratch_shapes=[
            pltpu.VMEM((X_SLOTS, TT, BATCH, D_MODEL), F32),
            pltpu.SemaphoreType.DMA((X_SLOTS, BATCH)),
            pltpu.VMEM((M, D_MODEL), BF16),
            pltpu.VMEM((M, D_MODEL), BF16),
            pltpu.VMEM((M + HALO_A, D_MODEL), F32),
            pltpu.VMEM((5, M, D_MODEL), F32),
            pltpu.VMEM((M + HALO_B, D_MODEL), F32),
            pltpu.VMEM((2, M, D_MODEL), F32),
            pltpu.VMEM((M, D_MODEL), BF16),
            pltpu.VMEM((M, D_MODEL), BF16),
            pltpu.VMEM((BATCH, D_MODEL), F32),
            pltpu.VMEM((D_MODEL, 6 * D_MODEL), BF16),
            pltpu.VMEM((D_MODEL, D_MODEL), BF16),
            pltpu.VMEM((D_MODEL, D_MODEL), BF16),
            pltpu.VMEM((D_MODEL, D_MODEL), BF16),
            pltpu.VMEM((2, W_CHUNK_ROWS, W_CHUNK_COLS), F32),
            pltpu.SemaphoreType.DMA((2,)),
        ],
        compiler_params=pltpu.CompilerParams(
            dimension_semantics=("arbitrary",), vmem_limit_bytes=VMEM_LIMIT_BYTES),
        name="mixer",
    )(*operands)


def _ffn_call(x1, gffn, wup, w3, b3, wdown, gfin):
    operands = (x1, x1, gffn, wup, w3, b3, wdown, gfin)
    in_hbm = {3, 6}
    in_specs = [_row_tile(), _row_tile(ahead=1)] + [
        pl.BlockSpec(memory_space=pl.ANY) if i in in_hbm else _resident(a.shape)
        for i, a in enumerate(operands) if i >= 2]
    return pl.pallas_call(
        _ffn_kernel,
        grid=(N_STEPS,),
        in_specs=in_specs,
        out_specs=pl.BlockSpec(memory_space=pl.ANY),
        out_shape=jax.ShapeDtypeStruct((BATCH, SEQ, D_MODEL), F32),
        scratch_shapes=[
            pltpu.VMEM((TT, BATCH, D_MODEL), F32),
            pltpu.SemaphoreType.DMA((BATCH,)),
            pltpu.VMEM((M, D_MODEL), F32),
            pltpu.VMEM((M, D_MODEL), BF16),
            pltpu.VMEM((M + HALO_F, D_FF), F32),
            pltpu.VMEM((M, D_FF), F32),
            pltpu.VMEM((M, D_FF), BF16),
            pltpu.VMEM((D_MODEL, 2 * D_FF), BF16),
            pltpu.VMEM((D_FF, D_MODEL), BF16),
            pltpu.VMEM((2, W_CHUNK_ROWS, W_CHUNK_COLS), F32),
            pltpu.SemaphoreType.DMA((2,)),
        ],
        compiler_params=pltpu.CompilerParams(
            dimension_semantics=("arbitrary",), vmem_limit_bytes=VMEM_LIMIT_BYTES),
        name="ffn",
    )(*operands)


def kernel(x, g_mix, w_in, lru_conv_w, lru_conv_b, lru_wa, lru_ba, lru_wx, lru_bx, lru_lambda,
           w_lru_out, conf_dw_w, conf_dw_b, conf_ln_g, conf_ln_b, w_conf_out, b_gate, w_out,
           g_ffn, w_up, ffn_dw_w, ffn_dw_b, w_down, g_final):
    assert x.shape == (BATCH, SEQ, D_MODEL)
    assert g_mix.shape[0] == 1
    row = lambda v: v.reshape(1, -1)
    l = 0

    wg = (0.5 * jnp.concatenate(
        [_block_diag_groups(lru_wa[l]), _block_diag_groups(lru_wx[l])], axis=-1)).astype(BF16)
    w31 = jnp.repeat(0.5 * conf_dw_w[l], BATCH, axis=0)
    x1 = _mixer_call(
        x, row(g_mix[l]), w_in[l], lru_conv_w[l], row(lru_conv_b[l]), wg,
        row(0.5 * lru_ba[l]), row(0.5 * lru_bx[l]), row(lru_lambda[l]),
        w_lru_out[l], w31, row(conf_dw_b[l]), row(conf_ln_g[l]),
        row(conf_ln_b[l]), w_conf_out[l], row(0.5 * b_gate[l]), w_out[l])
    return _ffn_call(
        x1, row(g_ffn[l]), w_up[l], ffn_dw_w[l], row(ffn_dw_b[l]), w_down[l], row(g_final))
```
